```python
import math
import jax, jax.numpy as jnp
from jax import lax
import numpy as np

D_MODEL = 4096
BATCH = 8
SEQ = 2048
DEPTH = 1

MEM_LEN = 256
DIFF_WIDTH = D_MODEL // 2
DN_WIDTH = D_MODEL - DIFF_WIDTH
DIFF_VDIM = 128
DIFF_HEADS = DIFF_WIDTH // DIFF_VDIM
DIFF_QKDIM = DIFF_VDIM // 2
DN_KDIM = 128
DN_VDIM = 128
DN_HEADS = DN_WIDTH // DN_VDIM
DN_CONV = 5
DN_CHUNK = 64
ROPE_THETA = 500000.0
ROPE_DIM = DIFF_QKDIM // 4
ATTN_BLOCK = 128
D_FF = ((8 * D_MODEL // 3 + 255) // 256) * 256
MEM_HEADS = 4
MEM_HDIM = 128
NORM_EPS = 1e-6
SUBLN_EPS = 1e-5
IN_SPLITS = [DIFF_WIDTH, 2 * DIFF_WIDTH, 3 * DIFF_WIDTH,
             3 * DIFF_WIDTH + 3 * DN_WIDTH, 3 * DIFF_WIDTH + 4 * DN_WIDTH]
IN_COLS = 3 * DIFF_WIDTH + 4 * DN_WIDTH + 4 * DN_HEADS

kernel_name = "hybrid_diffattn_gdn_macaron_encoder"

F32 = jnp.float32


def rms_norm(x, w, eps=NORM_EPS):
    xf = x.astype(F32)
    y = xf * lax.rsqrt(jnp.mean(xf * xf, axis=-1, keepdims=True) + eps)
    return (y * w.astype(F32)).astype(x.dtype)


def l2_norm(x, eps=1e-6):
    return x * lax.rsqrt(jnp.sum(x * x, axis=-1, keepdims=True) + eps)


def swiglu(h, w_gu, w_down):
    gate, up = jnp.split(h @ w_gu, 2, axis=-1)
    return (jax.nn.silu(gate) * up) @ w_down


def rope_tables(positions):
    inv_freq = ROPE_THETA ** (-jnp.arange(0, ROPE_DIM, 2, dtype=F32) / ROPE_DIM)
    ang = positions.astype(F32)[..., None] * inv_freq
    return jnp.cos(ang)[:, :, None, :], jnp.sin(ang)[:, :, None, :]


def apply_partial_rope(x, cos, sin):
    half = ROPE_DIM // 2
    xf = x.astype(F32)
    x1, x2, rest = xf[..., :half], xf[..., half:ROPE_DIM], xf[..., ROPE_DIM:]
    out = jnp.concatenate([x1 * cos - x2 * sin, x2 * cos + x1 * sin, rest], axis=-1)
    return out.astype(x.dtype)


def diff_attention(q, k, v, cos, sin, lam_params, subln_w, lambda_init):
    B, S = q.shape[0], q.shape[1]
    H, dq, dv = DIFF_HEADS, DIFF_QKDIM, DIFF_VDIM
    q = apply_partial_rope(q, cos, sin) * (dq ** -0.5)
    k = apply_partial_rope(k, cos, sin)
    lp = lam_params.astype(F32)
    lam = jnp.exp(jnp.sum(lp[0] * lp[1])) - jnp.exp(jnp.sum(lp[2] * lp[3])) + lambda_init
    nb = S // ATTN_BLOCK
    qb = q.reshape(B, nb, ATTN_BLOCK, 2 * H, dq).transpose(1, 0, 3, 2, 4)
    kt = k.transpose(0, 2, 1, 3)
    vt = v.transpose(0, 2, 1, 3)

    def block(qblk):
        s = jnp.einsum('bhqd,bhkd->bhqk', qblk, kt).astype(F32)
        p = jax.nn.softmax(s, axis=-1).reshape(B, H, 2, ATTN_BLOCK, S)
        p = p[:, :, 0] - lam * p[:, :, 1]
        return jnp.einsum('bhqk,bhkd->bhqd', p.astype(vt.dtype), vt)

    o = lax.map(block, qb)
    o = o.transpose(1, 0, 3, 2, 4).reshape(B, S, H, dv)
    o = rms_norm(o, subln_w, SUBLN_EPS) * (1.0 - lambda_init)
    return o.reshape(B, S, H * dv)


def short_conv(x, w):
    pad = (DN_CONV - 1) // 2
    y = lax.conv_general_dilated(x, w.astype(x.dtype)[:, None, :], window_strides=(1,),
                                 padding=[(pad, pad)], dimension_numbers=('NWC', 'WIO', 'NWC'),
                                 feature_group_count=x.shape[-1])
    return jax.nn.silu(y)


def gated_delta_chunked(q, k, v, g, beta):
    B, H, S, dk = q.shape
    dv = v.shape[-1]
    C = DN_CHUNK
    N = S // C
    q = q * (dk ** -0.5)
    qc = q.reshape(B, H, N, C, dk)
    kc = k.reshape(B, H, N, C, dk)
    vc = v.reshape(B, H, N, C, dv)
    bc = beta.reshape(B, H, N, C)
    gc = jnp.cumsum(g.reshape(B, H, N, C), axis=-1)
    incl = jnp.tril(jnp.ones((C, C), dtype=bool))
    strict = jnp.tril(jnp.ones((C, C), dtype=bool), -1)
    gdiff = gc[..., :, None] - gc[..., None, :]
    decay = jnp.where(incl, jnp.exp(jnp.where(incl, gdiff, 0.0)), 0.0)
    kb = kc * bc[..., None]
    a = jnp.where(strict, jnp.einsum('bhnid,bhnjd->bhnij', kb, kc) * decay, 0.0)
    rhs = jnp.concatenate([vc * bc[..., None], kb * jnp.exp(gc)[..., None]], axis=-1)
    sol = lax.linalg.triangular_solve(a + jnp.eye(C, dtype=F32), rhs, left_side=True, lower=True)
    u, w = sol[..., :dv], sol[..., dv:]
    qk = jnp.einsum('bhnid,bhnjd->bhnij', qc, kc) * decay
    g_last = gc[..., -1]
    k_tail = kc * jnp.exp(g_last[..., None] - gc)[..., None]

    def step(state, xs):
        q_n, u_n, w_n, qk_n, g_n, kt_n, gl_n = xs
        v_new = u_n - jnp.einsum('bhck,bhkv->bhcv', w_n, state)
        o = (jnp.einsum('bhck,bhkv->bhcv', q_n * jnp.exp(g_n)[..., None], state)
             + jnp.einsum('bhij,bhjv->bhiv', qk_n, v_new))
        state = state * jnp.exp(gl_n)[..., None, None] + jnp.einsum('bhck,bhcv->bhkv', kt_n, v_new)
        return state, o

    xs = tuple(jnp.moveaxis(t, 2, 0) for t in (qc, u, w, qk, gc, k_tail, g_last))
    state0 = jnp.zeros((B, H, dk, dv), F32)
    _, o = lax.scan(step, state0, xs)
    return jnp.moveaxis(o, 0, 2).reshape(B, H, S, dv)


def gated_deltanet_bidir(qkv, z, gates, conv_w, a_log, dt_bias, norm_w):
    B, S = qkv.shape[0], qkv.shape[1]
    H = DN_HEADS
    qkv = short_conv(qkv, conv_w)
    q, k, v = jnp.split(qkv, 3, axis=-1)
    to_heads = lambda t, d: t.reshape(B, S, H, d).transpose(0, 2, 1, 3).astype(F32)
    q = l2_norm(to_heads(q, DN_KDIM))
    k = l2_norm(to_heads(k, DN_KDIM))
    v = to_heads(v, DN_VDIM)
    gt = gates.astype(F32).reshape(B, S, 4, H).transpose(2, 0, 3, 1)
    A = jnp.exp(a_log.astype(F32))[:, None, :, None]
    dtb = dt_bias.astype(F32)[:, None, :, None]
    g_f = -A[0] * jax.nn.softplus(gt[0] + dtb[0])
    g_b = -A[1] * jax.nn.softplus(gt[2] + dtb[1])
    beta_f = jax.nn.sigmoid(gt[1])
    beta_b = jax.nn.sigmoid(gt[3])
    o_f = gated_delta_chunked(q, k, v, g_f, beta_f)
    flip = lambda t: jnp.flip(t, axis=2)
    o_b = flip(gated_delta_chunked(flip(q), flip(k), flip(v), flip(g_b), flip(beta_b)))
    o = (o_f + o_b).transpose(0, 2, 1, 3)
    zf = z.astype(F32).reshape(B, S, H, DN_VDIM)
    o = rms_norm(o, norm_w) * jax.nn.silu(zf)
    return o.reshape(B, S, H * DN_VDIM).astype(z.dtype)


def memory_cross_attention(xn, memn, w_q, w_kv, w_o):
    B, S = xn.shape[0], xn.shape[1]
    M = memn.shape[1]
    q = (xn @ w_q).reshape(B, S, MEM_HEADS, MEM_HDIM)
    kv = (memn @ w_kv).reshape(B, M, 2, MEM_HEADS, MEM_HDIM)
    k, v = kv[:, :, 0], kv[:, :, 1]
    s = jnp.einsum('bshd,bmhd->bhsm', q, k).astype(F32) * (MEM_HDIM ** -0.5)
    p = jax.nn.softmax(s, axis=-1)
    o = jnp.einsum('bhsm,bmhd->bshd', p.astype(v.dtype), v).reshape(B, S, MEM_HEADS * MEM_HDIM)
    return o @ w_o


def setup_inputs(seed: int = 0) -> dict:
    key = jax.random.key(seed)
    ks = jax.random.split(key, 24)
    nrm = lambda k, shape, scale: jax.random.normal(k, shape, F32) * scale
    gain = lambda k, shape: 1.0 + 0.05 * jax.random.normal(k, shape, F32)
    L, D = DEPTH, D_MODEL
    x = jax.random.normal(ks[0], (BATCH, SEQ, D), F32)
    mem = jax.random.normal(ks[1], (BATCH, MEM_LEN, D), F32)
    positions = (jnp.arange(SEQ, dtype=jnp.int32)[None, :]
                 + jax.random.randint(ks[2], (BATCH, 1), 0, 4096, dtype=jnp.int32))
    A = jax.random.uniform(ks[10], (L, 2, DN_HEADS), F32, 1.0, 16.0)
    dt = jnp.exp(jax.random.uniform(ks[11], (L, 2, DN_HEADS), F32, math.log(1e-3), math.log(1e-1)))
    return {
        "x": x,
        "mem": mem,
        "positions": positions,
        "ffn1_norms": gain(ks[3], (L, 2, D)),
        "ffn1_w_gu": nrm(ks[4], (L, D, 2 * D_FF), D ** -0.5),
        "ffn1_w_down": nrm(ks[5], (L, D_FF, D), D_FF ** -0.5),
        "mix_norms": gain(ks[6], (L, 2, D)),
        "mix_w_in": nrm(ks[7], (L, D, IN_COLS), D ** -0.5),
        "dn_conv_w": nrm(ks[8], (L, DN_CONV, 3 * DN_WIDTH), DN_CONV ** -0.5),
        "dn_a_log": jnp.log(A),
        "dn_dt_bias": dt + jnp.log(-jnp.expm1(-dt)),
        "dn_norm_w": gain(ks[12], (L, DN_VDIM)),
        "diff_lambda": nrm(ks[13], (L, 4, DIFF_QKDIM), 0.1),
        "diff_subln_w": gain(ks[14], (L, DIFF_VDIM)),
        "mix_w_out": nrm(ks[15], (L, D, D), D ** -0.5),
        "mem_norms": gain(ks[16], (L, 3, D)),
        "mem_w_q": nrm(ks[17], (L, D, MEM_HEADS * MEM_HDIM), D ** -0.5),
        "mem_w_kv": nrm(ks[18], (L, D, 2 * MEM_HEADS * MEM_HDIM), D ** -0.5),
        "mem_w_o": nrm(ks[19], (L, MEM_HEADS * MEM_HDIM, D), (MEM_HEADS * MEM_HDIM) ** -0.5),
        "ffn2_norms": gain(ks[20], (L, 2, D)),
        "ffn2_w_gu": nrm(ks[21], (L, D, 2 * D_FF), D ** -0.5),
        "ffn2_w_down": nrm(ks[22], (L, D_FF, D), D_FF ** -0.5),
    }


def reference(x, mem, positions, ffn1_norms, ffn1_w_gu, ffn1_w_down, mix_norms, mix_w_in,
              dn_conv_w, dn_a_log, dn_dt_bias, dn_norm_w, diff_lambda, diff_subln_w, mix_w_out,
              mem_norms, mem_w_q, mem_w_kv, mem_w_o, ffn2_norms, ffn2_w_gu, ffn2_w_down):
    B, S = x.shape[0], x.shape[1]
    cos, sin = rope_tables(positions)
    for l in range(DEPTH):
        lambda_init = 0.8 - 0.6 * math.exp(-0.3 * l)
        h = rms_norm(x, ffn1_norms[l, 0])
        x = x + 0.5 * rms_norm(swiglu(h, ffn1_w_gu[l], ffn1_w_down[l]), ffn1_norms[l, 1])
        h = rms_norm(x, mix_norms[l, 0])
        proj = h @ mix_w_in[l]
        d_q, d_k, d_v, dn_qkv, dn_z, dn_gates = jnp.split(proj, IN_SPLITS, axis=-1)
        o_diff = diff_attention(d_q.reshape(B, S, 2 * DIFF_HEADS, DIFF_QKDIM),
                                d_k.reshape(B, S, 2 * DIFF_HEADS, DIFF_QKDIM),
                                d_v.reshape(B, S, DIFF_HEADS, DIFF_VDIM),
                                cos, sin, diff_lambda[l], diff_subln_w[l], lambda_init)
        o_dn = gated_deltanet_bidir(dn_qkv, dn_z, dn_gates, dn_conv_w[l], dn_a_log[l],
                                    dn_dt_bias[l], dn_norm_w[l])
        mixed = jnp.concatenate([o_diff.astype(x.dtype), o_dn.astype(x.dtype)], axis=-1) @ mix_w_out[l]
        x = x + rms_norm(mixed, mix_norms[l, 1])
        h = rms_norm(x, mem_norms[l, 0])
        memn = rms_norm(mem, mem_norms[l, 1])
        c = memory_cross_attention(h, memn, mem_w_q[l], mem_w_kv[l], mem_w_o[l])
        x = x + rms_norm(c, mem_norms[l, 2])
        h = rms_norm(x, ffn2_norms[l, 0])
        x = x + 0.5 * rms_norm(swiglu(h, ffn2_w_gu[l], ffn2_w_down[l]), ffn2_norms[l, 1])
    return x
```

```python
import functools
import math

import jax
import jax.numpy as jnp
from jax import lax
from jax.experimental import pallas as pl
from jax.experimental.pallas import tpu as pltpu

F32 = jnp.float32
BF16 = jnp.bfloat16

DIFF_VDIM = 128
DIFF_QKDIM = 64
DN_KDIM = 128
DN_VDIM = 128
DN_CONV = 5
DN_CHUNK = 64
DN_BLOCK = 16
ROPE_THETA = 500000.0
ROPE_DIM = DIFF_QKDIM // 4
MEM_HEADS = 4
MEM_HDIM = 128
NORM_EPS = 1e-6
SUBLN_EPS = 1e-5
LANES = 128
FF_ALIGN = 1024
VMEM_LIMIT = 56 * 1024 * 1024


def _cparams(*sem):
    return pltpu.CompilerParams(dimension_semantics=sem, vmem_limit_bytes=VMEM_LIMIT)


def _dot(a, b):
    return jnp.dot(a, b, preferred_element_type=F32)


def _dot_nt(a, b):
    return lax.dot_general(a, b, (((1,), (1,)), ((), ())), preferred_element_type=F32)


def _dot_tn(a, b):
    return lax.dot_general(a, b, (((0,), (0,)), ((), ())), preferred_element_type=F32)


def _dot_exact(a, b):
    return jnp.dot(a, b, preferred_element_type=F32, precision=lax.Precision.HIGHEST)


def _rms(x, w, eps):
    return x * lax.rsqrt(jnp.mean(x * x, axis=-1, keepdims=True) + eps) * w


def _silu(x):
    return x * jax.nn.sigmoid(x)


def _norm_kernel(x_ref, w_ref, o_ref):
    o_ref[...] = _rms(x_ref[...], w_ref[...], NORM_EPS).astype(o_ref.dtype)


def _norm_cast(x, w, tm=256):
    T, D = x.shape
    return pl.pallas_call(
        _norm_kernel,
        grid=(T // tm,),
        in_specs=[pl.BlockSpec((tm, D), lambda i: (i, 0)),
                  pl.BlockSpec((1, D), lambda i: (0, 0))],
        out_specs=pl.BlockSpec((tm, D), lambda i: (i, 0)),
        out_shape=jax.ShapeDtypeStruct((T, D), BF16),
        compiler_params=_cparams("parallel"),
        name="norm_cast",
    )(x, w.reshape(1, D))


def _post_pre_kernel(y_ref, x_ref, wpost_ref, wpre_ref, xo_ref, h_ref, *, scale):
    xn = x_ref[...] + scale * _rms(y_ref[...], wpost_ref[...], NORM_EPS)
    xo_ref[...] = xn
    h_ref[...] = _rms(xn, wpre_ref[...], NORM_EPS).astype(h_ref.dtype)


def _post_kernel(y_ref, x_ref, wpost_ref, xo_ref, *, scale):
    xo_ref[...] = x_ref[...] + scale * _rms(y_ref[...], wpost_ref[...], NORM_EPS)


def _post_pre(y, x, w_post, w_pre, scale, tm=256):
    T, D = x.shape
    row = pl.BlockSpec((tm, D), lambda i: (i, 0))
    vec = pl.BlockSpec((1, D), lambda i: (0, 0))
    if w_pre is None:
        return pl.pallas_call(
            functools.partial(_post_kernel, scale=scale),
            grid=(T // tm,),
            in_specs=[row, row, vec],
            out_specs=row,
            out_shape=jax.ShapeDtypeStruct((T, D), F32),
            compiler_params=_cparams("parallel"),
            name="post_norm_residual",
        )(y, x, w_post.reshape(1, D))
    return pl.pallas_call(
        functools.partial(_post_pre_kernel, scale=scale),
        grid=(T // tm,),
        in_specs=[row, row, vec, vec],
        out_specs=[row, row],
        out_shape=[jax.ShapeDtypeStruct((T, D), F32), jax.ShapeDtypeStruct((T, D), BF16)],
        compiler_params=_cparams("parallel"),
        name="post_norm_residual_pre_norm",
    )(y, x, w_post.reshape(1, D), w_pre.reshape(1, D))


def _mm_kernel(a_ref, w_ref, o_ref, *acc, nk):
    if nk == 1:
        o_ref[...] = _dot(a_ref[...], w_ref[...]).astype(o_ref.dtype)
        return
    acc_ref, = acc
    k = pl.program_id(2)

    @pl.when(k == 0)
    def _():
        acc_ref[...] = jnp.zeros_like(acc_ref)

    acc_ref[...] += _dot(a_ref[...], w_ref[...])

    @pl.when(k == nk - 1)
    def _():
        o_ref[...] = acc_ref[...].astype(o_ref.dtype)


def _matmul(a, w, out_dtype, tm, tn, tk=None, name="matmul"):
    M, K = a.shape
    N = w.shape[1]
    tk = K if tk is None else tk
    tm, tn = min(tm, M), min(tn, N)
    nk = K // tk
    assert M % tm == 0 and N % tn == 0 and K % tk == 0
    return pl.pallas_call(
        functools.partial(_mm_kernel, nk=nk),
        grid=(M // tm, N // tn, nk),
        in_specs=[pl.BlockSpec((tm, tk), lambda i, j, k: (i, k)),
                  pl.BlockSpec((tk, tn), lambda i, j, k: (k, j))],
        out_specs=pl.BlockSpec((tm, tn), lambda i, j, k: (i, j)),
        out_shape=jax.ShapeDtypeStruct((M, N), out_dtype),
        scratch_shapes=[pltpu.VMEM((tm, tn), F32)] if nk > 1 else [],
        compiler_params=_cparams("parallel", "parallel", "arbitrary"),
        name=name,
    )(a, w)


def _mm_swiglu_kernel(a_ref, wg_ref, wu_ref, o_ref):
    a = a_ref[...]
    g = _dot(a, wg_ref[...])
    u = _dot(a, wu_ref[...])
    o_ref[...] = (_silu(g) * u).astype(o_ref.dtype)


def _matmul_swiglu(a, w_gate, w_up, tm, tn):
    M, K = a.shape
    N = w_gate.shape[1]
    assert M % tm == 0 and N % tn == 0
    wspec = pl.BlockSpec((K, tn), lambda i, j: (0, j))
    return pl.pallas_call(
        _mm_swiglu_kernel,
        grid=(M // tm, N // tn),
        in_specs=[pl.BlockSpec((tm, K), lambda i, j: (i, 0)), wspec, wspec],
        out_specs=pl.BlockSpec((tm, tn), lambda i, j: (i, j)),
        out_shape=jax.ShapeDtypeStruct((M, N), BF16),
        compiler_params=_cparams("parallel", "parallel"),
        name="matmul_swiglu",
    )(a, w_gate, w_up)


def _mm_dual_kernel(a1_ref, a2_ref, w1_ref, w2_ref, o_ref):
    o_ref[...] = (_dot(a1_ref[...], w1_ref[...]) + _dot(a2_ref[...], w2_ref[...])).astype(o_ref.dtype)


def _matmul_dual(a1, a2, w1, w2, tm, tn):
    M, K1 = a1.shape
    K2 = a2.shape[1]
    N = w1.shape[1]
    assert M % tm == 0 and N % tn == 0
    return pl.pallas_call(
        _mm_dual_kernel,
        grid=(M // tm, N // tn),
        in_specs=[pl.BlockSpec((tm, K1), lambda i, j: (i, 0)),
                  pl.BlockSpec((tm, K2), lambda i, j: (i, 0)),
                  pl.BlockSpec((K1, tn), lambda i, j: (0, j)),
                  pl.BlockSpec((K2, tn), lambda i, j: (0, j))],
        out_specs=pl.BlockSpec((tm, tn), lambda i, j: (i, j)),
        out_shape=jax.ShapeDtypeStruct((M, N), F32),
        compiler_params=_cparams("parallel", "parallel"),
        name="matmul_dual",
    )(a1, a2, w1, w2)


def _ffn(h, w_gate, w_up, w_down):
    act = _matmul_swiglu(h, w_gate, w_up, tm=1024, tn=512)
    kf = w_down.shape[0]
    return _matmul(act, w_down, F32, tm=1024, tn=1024, tk=kf // 4, name="matmul_down")


def _diff_attn_kernel(lam_ref, q_ref, k_ref, v_ref, c_ref, sa_ref, sb_ref, w_ref, o_ref,
                      qs_ref, ks_ref, vs_ref, *, tq, lambda_init):
    S = q_ref.shape[0]
    c, sa, sb = c_ref[...], sa_ref[...], sb_ref[...]

    def rope(x):
        half = ROPE_DIM // 2
        return x * c + pltpu.roll(x, half, 1) * sa + pltpu.roll(x, LANES - half, 1) * sb

    q = rope(q_ref[...].astype(F32)) * (DIFF_QKDIM ** -0.5)
    k = rope(k_ref[...].astype(F32))
    first_map = lax.broadcasted_iota(jnp.int32, (S, LANES), 1) < DIFF_QKDIM
    qs_ref[0] = jnp.where(first_map, q, 0.0).astype(BF16)
    qs_ref[1] = jnp.where(first_map, 0.0, q).astype(BF16)
    ks_ref[...] = k.astype(BF16)
    vs_ref[...] = v_ref[...].astype(BF16)

    lp = lam_ref[...]
    lam = (jnp.exp(jnp.sum(lp[0:1] * lp[1:2], keepdims=True))
           - jnp.exp(jnp.sum(lp[2:3] * lp[3:4], keepdims=True)) + lambda_init)
    w = w_ref[...]

    def softmax_pv(qh):
        s = _dot_nt(qh, ks_ref[...])
        e = jnp.exp(s - jnp.max(s, axis=-1, keepdims=True))
        l = jnp.sum(e, axis=-1, keepdims=True)
        return _dot(e.astype(BF16), vs_ref[...]) / l

    def body(i, carry):
        r = pl.multiple_of(i * tq, tq)
        o = softmax_pv(qs_ref[0, pl.ds(r, tq), :]) - lam * softmax_pv(qs_ref[1, pl.ds(r, tq), :])
        o = _rms(o, w, SUBLN_EPS) * (1.0 - lambda_init)
        o_ref[pl.ds(r, tq), :] = o.astype(o_ref.dtype)
        return carry

    lax.fori_loop(0, S // tq, body, 0)


def _diff_attention(proj, rope_c, rope_sa, rope_sb, lam_params, subln_w, heads, lambda_init, tq=256):
    B, S, _ = proj.shape
    tq = min(tq, S)
    head = lambda off: pl.BlockSpec((None, S, LANES), lambda b, h: (b, 0, off + h))
    table = pl.BlockSpec((None, S, LANES), lambda b, h: (b, 0, 0))
    return pl.pallas_call(
        functools.partial(_diff_attn_kernel, tq=tq, lambda_init=lambda_init),
        grid=(B, heads),
        in_specs=[pl.BlockSpec((4, DIFF_QKDIM), lambda b, h: (0, 0)),
                  head(0), head(heads), head(2 * heads), table, table, table,
                  pl.BlockSpec((1, DIFF_VDIM), lambda b, h: (0, 0))],
        out_specs=pl.BlockSpec((None, S, LANES), lambda b, h: (b, 0, h)),
        out_shape=jax.ShapeDtypeStruct((B, S, heads * DIFF_VDIM), BF16),
        scratch_shapes=[pltpu.VMEM((2, S, LANES), BF16), pltpu.VMEM((S, LANES), BF16),
                        pltpu.VMEM((S, LANES), BF16)],
        compiler_params=_cparams("parallel", "parallel"),
        name="diff_attention",
    )(lam_params, proj, proj, proj, rope_c, rope_sa, rope_sb, subln_w.reshape(1, DIFF_VDIM))


def _unit_tri_inverse(a, eye, same_block):
    mm = lambda x, y: _dot(x.astype(BF16), y.astype(BF16))
    ad = jnp.where(same_block, a, 0.0)
    ao = a - ad
    p = eye - ad
    sq = ad
    for _ in range(int(math.log2(DN_BLOCK)) - 1):
        sq = mm(sq, sq)
        p = p + mm(p, sq)
    b = mm(p, ao)
    b2 = mm(b, b)
    x = eye - b + b2 - mm(b, b2)
    return mm(x, p)


def _gdn_kernel(alog_ref, dtb_ref, q_ref, k_ref, v_ref, z_ref, wq_ref, wk_ref, wv_ref,
                gcol_ref, grow_ref, nw_ref, o_ref,
                qn_ref, kn_ref, vn_ref, u_ref, w_ref, qg_ref, kt_ref, qkd_ref, egl_ref,
                gcr_ref, od_ref):
    S = q_ref.shape[0]
    C = DN_CHUNK
    N = S // C
    h = pl.program_id(1)

    t = lax.broadcasted_iota(jnp.int32, (S, LANES), 0)
    pad = (DN_CONV - 1) // 2

    def conv_silu(x_ref, cw_ref):
        x = x_ref[...].astype(F32)
        cw = cw_ref[...].astype(F32)
        y = x * cw[pad:pad + 1]
        for i in range(DN_CONV):
            d = i - pad
            if d == 0:
                continue
            xs = pltpu.roll(x, (-d) % S, 0)
            valid = (t + d >= 0) & (t + d < S)
            y = y + jnp.where(valid, xs, 0.0) * cw[i:i + 1]
        return _silu(y)

    def l2n(x):
        return x * lax.rsqrt(jnp.sum(x * x, axis=-1, keepdims=True) + 1e-6)

    qn_ref[...] = l2n(conv_silu(q_ref, wq_ref)) * (DN_KDIM ** -0.5)
    kn_ref[...] = l2n(conv_silu(k_ref, wk_ref))
    vn_ref[...] = conv_silu(v_ref, wv_ref)

    ri = lax.broadcasted_iota(jnp.int32, (C, C), 0)
    ci = lax.broadcasted_iota(jnp.int32, (C, C), 1)
    eye = (ri == ci).astype(F32)
    same_block = (ri // DN_BLOCK) == (ci // DN_BLOCK)
    incl = (ri >= ci, ri <= ci)
    strict = (ri > ci, ri < ci)
    tri_col = (incl[0].astype(F32), incl[1].astype(F32))
    tri_row = (incl[1].astype(F32), incl[0].astype(F32))

    def log_decay(x, d):
        a = jnp.exp(jnp.full((1, 1), alog_ref[d, h], F32))
        xb = x + dtb_ref[d, h]
        return -a * (jnp.maximum(xb, 0.0) + jnp.log1p(jnp.exp(-jnp.abs(xb))))

    for d in range(2):
        g_row = log_decay(grow_ref[2 * d].astype(F32), d)
        gcr_ref[d] = _dot_exact(g_row, tri_row[d])

    def chunk_factors(n, carry):
        r = pl.multiple_of(n * C, C)
        rows = pl.ds(r, C)
        qc, kc, vc = qn_ref[rows, :], kn_ref[rows, :], vn_ref[rows, :]
        k16 = kc.astype(BF16)
        kk = _dot_nt(k16, k16)
        qk = _dot_nt(qc.astype(BF16), k16)
        gl = gcol_ref[rows, :].astype(F32)
        for d in range(2):
            a_col = jnp.broadcast_to(gl[:, 2 * d:2 * d + 1], (C, LANES))
            b_col = jnp.broadcast_to(gl[:, 2 * d + 1:2 * d + 2], (C, LANES))
            beta = jax.nn.sigmoid(b_col)
            gc = _dot_exact(tri_col[d], log_decay(a_col, d))
            gc_row = gcr_ref[d, pl.ds(n, 1), :]
            diff = jnp.where(incl[d], gc[:, :C] - gc_row, 0.0)
            decay = jnp.where(incl[d], jnp.exp(diff), 0.0)
            a = jnp.where(strict[d], beta[:, :C] * kk * decay, 0.0)
            tinv = _unit_tri_inverse(a, eye, same_block)
            egc = jnp.exp(gc)
            kbeta = kc * beta
            rhs = jnp.concatenate([vc * beta, kbeta * egc], axis=1)
            uw = _dot(tinv.astype(BF16), rhs.astype(BF16))
            u_ref[d, rows, :] = uw[:, :LANES]
            w_ref[d, rows, :] = uw[:, LANES:]
            qkd_ref[d, n] = qk * decay
            qg_ref[d, rows, :] = qc * egc
            g_last = gc[C - 1:C, :] if d == 0 else gc[0:1, :]
            kt_ref[d, rows, :] = kc * jnp.exp(g_last - gc)
            egl_ref[d, n] = jnp.broadcast_to(jnp.exp(g_last), (8, LANES))
        return carry

    lax.fori_loop(0, N, chunk_factors, 0)

    def scan_step(d, n, state):
        r = pl.multiple_of(n * C, C)
        rows = pl.ds(r, C)
        s16 = state.astype(BF16)
        v_new = u_ref[d, rows, :] - _dot(w_ref[d, rows, :].astype(BF16), s16)
        v16 = v_new.astype(BF16)
        od_ref[d, rows, :] = (_dot(qg_ref[d, rows, :].astype(BF16), s16)
                              + _dot(qkd_ref[d, n].astype(BF16), v16))
        return state * egl_ref[d, n][0:1, :] + _dot_tn(kt_ref[d, rows, :].astype(BF16), v16)

    def scan_body(n, states):
        return scan_step(0, n, states[0]), scan_step(1, N - 1 - n, states[1])

    zero = jnp.zeros((DN_KDIM, DN_VDIM), F32)
    lax.fori_loop(0, N, scan_body, (zero, zero))

    o = od_ref[0] + od_ref[1]
    o = _rms(o, nw_ref[...], NORM_EPS) * _silu(z_ref[...].astype(F32))
    o_ref[...] = o.astype(o_ref.dtype)


def _gated_deltanet(proj, qkv_block0, z_block0, conv_w, gates_col, gates_row, a_log, dt_bias, norm_w, heads):
    B, S, _ = proj.shape
    N = S // DN_CHUNK
    head = lambda off: pl.BlockSpec((None, S, LANES), lambda b, h: (b, 0, off + h))
    cw = lambda off: pl.BlockSpec((DN_CONV, LANES), lambda b, h: (0, off + h))
    smem = pl.BlockSpec(memory_space=pltpu.SMEM)
    return pl.pallas_call(
        _gdn_kernel,
        grid=(B, heads),
        in_specs=[smem, smem,
                  head(qkv_block0), head(qkv_block0 + heads), head(qkv_block0 + 2 * heads), head(z_block0),
                  cw(0), cw(heads), cw(2 * heads),
                  pl.BlockSpec((None, None, S, 4), lambda b, h: (b, h, 0, 0)),
                  pl.BlockSpec((None, None, 4, N, DN_CHUNK), lambda b, h: (b, h, 0, 0, 0)),
                  pl.BlockSpec((1, DN_VDIM), lambda b, h: (0, 0))],
        out_specs=pl.BlockSpec((None, S, LANES), lambda b, h: (b, 0, h)),
        out_shape=jax.ShapeDtypeStruct((B, S, heads * DN_VDIM), BF16),
        scratch_shapes=[pltpu.VMEM((S, LANES), F32)] * 3
        + [pltpu.VMEM((2, S, LANES), F32)] * 4
        + [pltpu.VMEM((2, N, DN_CHUNK, DN_CHUNK), F32), pltpu.VMEM((2, N, 8, LANES), F32),
           pltpu.VMEM((2, N, DN_CHUNK), F32), pltpu.VMEM((2, S, LANES), F32)],
        compiler_params=_cparams("parallel", "parallel"),
        name="gated_deltanet",
    )(a_log, dt_bias, proj, proj, proj, proj, conv_w, conv_w, conv_w, gates_col, gates_row,
      norm_w.reshape(1, DN_VDIM))


def _cross_attn_kernel(q_ref, kv_ref, o_ref):
    width = MEM_HEADS * MEM_HDIM
    outs = []
    for hd in range(MEM_HEADS):
        cols = slice(hd * MEM_HDIM, (hd + 1) * MEM_HDIM)
        q = q_ref[:, cols]
        k = kv_ref[:, cols]
        v = kv_ref[:, width + hd * MEM_HDIM:width + (hd + 1) * MEM_HDIM]
        s = _dot_nt(q, k) * (MEM_HDIM ** -0.5)
        e = jnp.exp(s - jnp.max(s, axis=-1, keepdims=True))
        l = jnp.sum(e, axis=-1, keepdims=True)
        outs.append(_dot(e.astype(BF16), v) / l)
    o_ref[...] = jnp.concatenate(outs, axis=1).astype(o_ref.dtype)


def _cross_attention(q, kv, tq=512):
    B, S, W = q.shape
    M = kv.shape[1]
    tq = min(tq, S)
    return pl.pallas_call(
        _cross_attn_kernel,
        grid=(B, S // tq),
        in_specs=[pl.BlockSpec((None, tq, W), lambda b, i: (b, i, 0)),
                  pl.BlockSpec((None, M, 2 * W), lambda b, i: (b, 0, 0))],
        out_specs=pl.BlockSpec((None, tq, W), lambda b, i: (b, i, 0)),
        out_shape=jax.ShapeDtypeStruct((B, S, W), BF16),
        compiler_params=_cparams("parallel", "parallel"),
        name="cross_attention",
    )(q, kv)


def _rope_tables(positions):
    B, S = positions.shape
    half = ROPE_DIM // 2
    inv_freq = ROPE_THETA ** (-jnp.arange(0, ROPE_DIM, 2, dtype=F32) / ROPE_DIM)
    ang = positions.astype(F32)[..., None] * inv_freq
    cos, sin = jnp.cos(ang), jnp.sin(ang)
    rest = DIFF_QKDIM - ROPE_DIM
    ones, zeros = jnp.ones((B, S, rest), F32), jnp.zeros((B, S, rest), F32)
    z8 = jnp.zeros((B, S, half), F32)
    both = lambda m: jnp.concatenate([m, m], axis=-1)
    c = both(jnp.concatenate([cos, cos, ones], axis=-1))
    sa = both(jnp.concatenate([z8, sin, zeros], axis=-1))
    sb = both(jnp.concatenate([-sin, z8, zeros], axis=-1))
    return c, sa, sb


def _pad_cols(w, n):
    return jnp.pad(w, ((0, 0), (0, n - w.shape[1])))


def _ffn_weights(w_gu, w_down):
    d_ff = w_down.shape[0]
    d_pad = -(-d_ff // FF_ALIGN) * FF_ALIGN
    w_gate = _pad_cols(w_gu[:, :d_ff], d_pad).astype(BF16)
    w_up = _pad_cols(w_gu[:, d_ff:], d_pad).astype(BF16)
    w_dn = jnp.pad(w_down, ((0, d_pad - d_ff), (0, 0))).astype(BF16)
    return w_gate, w_up, w_dn


def kernel(x, mem, positions, ffn1_norms, ffn1_w_gu, ffn1_w_down, mix_norms, mix_w_in, dn_conv_w, dn_a_log, dn_dt_bias, dn_norm_w, diff_lambda, diff_subln_w, mix_w_out, mem_norms, mem_w_q, mem_w_kv, mem_w_o, ffn2_norms, ffn2_w_gu, ffn2_w_down):
    B, S, D = x.shape
    T = B * S
    depth = ffn1_norms.shape[0]
    diff_width = D // 2
    dn_width = D - diff_width
    diff_heads = diff_width // DIFF_VDIM
    dn_heads = dn_width // DN_VDIM
    main_cols = 3 * diff_width + 4 * dn_width
    n_chunks = S // DN_CHUNK

    rope_c, rope_sa, rope_sb = _rope_tables(positions)
    xt = x.reshape(T, D)
    h = _norm_cast(xt, ffn1_norms[0, 0])
    for l in range(depth):
        lambda_init = 0.8 - 0.6 * math.exp(-0.3 * l)
        y = _ffn(h, *_ffn_weights(ffn1_w_gu[l], ffn1_w_down[l]))
        xt, h = _post_pre(y, xt, ffn1_norms[l, 1], mix_norms[l, 0], 0.5)
        w_in = mix_w_in[l]
        proj = _matmul(h, w_in[:, :main_cols].astype(BF16), F32, tm=1024, tn=1024, name="matmul_in")
        gates = _matmul(h, _pad_cols(w_in[:, main_cols:], LANES).astype(BF16), F32, tm=1024, tn=LANES,
                        name="matmul_gates")
        proj = proj.reshape(B, S, main_cols)
        gates = gates.reshape(B, S, LANES)[:, :, :4 * dn_heads].reshape(B, S, 4, dn_heads)
        gates_col = gates.transpose(0, 3, 1, 2)
        gates_row = gates.transpose(0, 3, 2, 1).reshape(B, dn_heads, 4, n_chunks, DN_CHUNK)
        o_diff = _diff_attention(proj, rope_c, rope_sa, rope_sb, diff_lambda[l], diff_subln_w[l],
                                 diff_heads, lambda_init)
        o_dn = _gated_deltanet(proj, 3 * diff_heads, 3 * diff_heads + 3 * dn_heads, dn_conv_w[l],
                               gates_col, gates_row, dn_a_log[l], dn_dt_bias[l], dn_norm_w[l], dn_heads)
        w_out = mix_w_out[l]
        mixed = _matmul_dual(o_diff.reshape(T, diff_width), o_dn.reshape(T, dn_width),
                             w_out[:diff_width].astype(BF16), w_out[diff_width:].astype(BF16),
                             tm=1024, tn=1024)
        xt, h = _post_pre(mixed, xt, mix_norms[l, 1], mem_norms[l, 0], 1.0)
        mlen = mem.shape[1]
        memn = _norm_cast(mem.reshape(B * mlen, D), mem_norms[l, 1])
        kv = _matmul(memn, mem_w_kv[l].astype(BF16), BF16, tm=1024, tn=1024, name="matmul_mem_kv")
        q = _matmul(h, mem_w_q[l].astype(BF16), BF16, tm=1024, tn=512, name="matmul_mem_q")
        width = MEM_HEADS * MEM_HDIM
        o_mem = _cross_attention(q.reshape(B, S, width), kv.reshape(B, mlen, 2 * width))
        c = _matmul(o_mem.reshape(T, width), mem_w_o[l].astype(BF16), F32, tm=1024, tn=1024,
                    name="matmul_mem_o")
        xt, h = _post_pre(c, xt, mem_norms[l, 2], ffn2_norms[l, 0], 1.0)
        y = _ffn(h, *_ffn_weights(ffn2_w_gu[l], ffn2_w_down[l]))
        if l + 1 < depth:
            xt, h = _post_pre(y, xt, ffn2_norms[l, 1], ffn1_norms[l + 1, 0], 0.5)
        else:
            xt = _post_pre(y, xt, ffn2_norms[l, 1], None, 0.5)
    return xt.reshape(B, S, D)
```

```python
import functools
import math

import jax
import jax.numpy as jnp
from jax import lax
from jax.experimental import pallas as pl
from jax.experimental.pallas import tpu as pltpu

F32 = jnp.float32
BF16 = jnp.bfloat16

DIFF_VDIM = 128
DIFF_QKDIM = 64
DN_KDIM = 128
DN_VDIM = 128
DN_CONV = 5
DN_CHUNK = 64
DN_BLOCK = 16
DN_UNROLL = 8
ROPE_THETA = 500000.0
ROPE_DIM = DIFF_QKDIM // 4
MEM_HEADS = 4
MEM_HDIM = 128
NORM_EPS = 1e-6
SUBLN_EPS = 1e-5
LANES = 128
FF_ALIGN = 1024
VMEM_LIMIT = 56 * 1024 * 1024


def _cparams(*sem):
    return pltpu.CompilerParams(dimension_semantics=sem, vmem_limit_bytes=VMEM_LIMIT)


def _dot(a, b):
    return jnp.dot(a, b, preferred_element_type=F32)


def _dot_nt(a, b):
    return lax.dot_general(a, b, (((1,), (1,)), ((), ())), preferred_element_type=F32)


def _dot_tn(a, b):
    return lax.dot_general(a, b, (((0,), (0,)), ((), ())), preferred_element_type=F32)


def _dot_exact(a, b):
    return jnp.dot(a, b, preferred_element_type=F32, precision=lax.Precision.HIGHEST)


def _rms(x, w, eps):
    return x * lax.rsqrt(jnp.mean(x * x, axis=-1, keepdims=True) + eps) * w


def _silu(x):
    return x * jax.nn.sigmoid(x)


def _norm_kernel(x_ref, w_ref, o_ref):
    o_ref[...] = _rms(x_ref[...], w_ref[...], NORM_EPS).astype(o_ref.dtype)


def _norm_cast(x, w, tm=256):
    T, D = x.shape
    return pl.pallas_call(
        _norm_kernel,
        grid=(T // tm,),
        in_specs=[pl.BlockSpec((tm, D), lambda i: (i, 0)),
                  pl.BlockSpec((1, D), lambda i: (0, 0))],
        out_specs=pl.BlockSpec((tm, D), lambda i: (i, 0)),
        out_shape=jax.ShapeDtypeStruct((T, D), BF16),
        compiler_params=_cparams("parallel"),
        name="norm_cast",
    )(x, w.reshape(1, D))


def _post_pre_kernel(y_ref, x_ref, wpost_ref, wpre_ref, xo_ref, h_ref, *, scale):
    xn = x_ref[...] + scale * _rms(y_ref[...].astype(F32), wpost_ref[...], NORM_EPS)
    xo_ref[...] = xn
    h_ref[...] = _rms(xn, wpre_ref[...], NORM_EPS).astype(h_ref.dtype)


def _post_kernel(y_ref, x_ref, wpost_ref, xo_ref, *, scale):
    xo_ref[...] = x_ref[...] + scale * _rms(y_ref[...].astype(F32), wpost_ref[...], NORM_EPS)


def _post_pre(y, x, w_post, w_pre, scale, tm=256):
    T, D = x.shape
    row = pl.BlockSpec((tm, D), lambda i: (i, 0))
    vec = pl.BlockSpec((1, D), lambda i: (0, 0))
    if w_pre is None:
        return pl.pallas_call(
            functools.partial(_post_kernel, scale=scale),
            grid=(T // tm,),
            in_specs=[row, row, vec],
            out_specs=row,
            out_shape=jax.ShapeDtypeStruct((T, D), F32),
            compiler_params=_cparams("parallel"),
            name="post_norm_residual",
        )(y, x, w_post.reshape(1, D))
    return pl.pallas_call(
        functools.partial(_post_pre_kernel, scale=scale),
        grid=(T // tm,),
        in_specs=[row, row, vec, vec],
        out_specs=[row, row],
        out_shape=[jax.ShapeDtypeStruct((T, D), F32), jax.ShapeDtypeStruct((T, D), BF16)],
        compiler_params=_cparams("parallel"),
        name="post_norm_residual_pre_norm",
    )(y, x, w_post.reshape(1, D), w_pre.reshape(1, D))


def _mm_kernel(a_ref, w_ref, o_ref, *acc, nk):
    if nk == 1:
        o_ref[...] = _dot(a_ref[...], w_ref[...]).astype(o_ref.dtype)
        return
    acc_ref, = acc
    k = pl.program_id(2)

    @pl.when(k == 0)
    def _():
        acc_ref[...] = jnp.zeros_like(acc_ref)

    acc_ref[...] += _dot(a_ref[...], w_ref[...])

    @pl.when(k == nk - 1)
    def _():
        o_ref[...] = acc_ref[...].astype(o_ref.dtype)


def _matmul(a, w, out_dtype, tm, tn, tk=None, n_cols=None, name="matmul"):
    M, K = a.shape
    N = w.shape[1] if n_cols is None else n_cols
    tk = K if tk is None else tk
    tm, tn = min(tm, M), min(tn, N)
    nk = K // tk
    assert M % tm == 0 and N % tn == 0 and K % tk == 0
    return pl.pallas_call(
        functools.partial(_mm_kernel, nk=nk),
        grid=(M // tm, N // tn, nk),
        in_specs=[pl.BlockSpec((tm, tk), lambda i, j, k: (i, k)),
                  pl.BlockSpec((tk, tn), lambda i, j, k: (k, j))],
        out_specs=pl.BlockSpec((tm, tn), lambda i, j, k: (i, j)),
        out_shape=jax.ShapeDtypeStruct((M, N), out_dtype),
        scratch_shapes=[pltpu.VMEM((tm, tn), F32)] if nk > 1 else [],
        compiler_params=_cparams("parallel", "parallel", "arbitrary"),
        name=name,
    )(a, w)


def _mm_swiglu_kernel(a_ref, wg_ref, wu_ref, o_ref):
    a = a_ref[...]
    g = _dot(a, wg_ref[...])
    u = _dot(a, wu_ref[...])
    o_ref[...] = (_silu(g) * u).astype(o_ref.dtype)


def _matmul_swiglu(a, w_gate, w_up, tm, tn):
    M, K = a.shape
    N = w_up.shape[1]
    tm = min(tm, M)
    assert M % tm == 0 and N % tn == 0 and w_gate.shape[1] >= N
    wspec = pl.BlockSpec((K, tn), lambda i, j: (0, j))
    return pl.pallas_call(
        _mm_swiglu_kernel,
        grid=(M // tm, N // tn),
        in_specs=[pl.BlockSpec((tm, K), lambda i, j: (i, 0)), wspec, wspec],
        out_specs=pl.BlockSpec((tm, tn), lambda i, j: (i, j)),
        out_shape=jax.ShapeDtypeStruct((M, N), BF16),
        compiler_params=_cparams("parallel", "parallel"),
        name="matmul_swiglu",
    )(a, w_gate, w_up)


def _mm_dual_kernel(a1_ref, a2_ref, w1_ref, w2_ref, o_ref):
    o_ref[...] = (_dot(a1_ref[...], w1_ref[...]) + _dot(a2_ref[...], w2_ref[...])).astype(o_ref.dtype)


def _matmul_dual(a1, a2, w1, w2, tm, tn):
    M, K1 = a1.shape
    K2 = a2.shape[1]
    N = w1.shape[1]
    tm = min(tm, M)
    assert M % tm == 0 and N % tn == 0
    return pl.pallas_call(
        _mm_dual_kernel,
        grid=(M // tm, N // tn),
        in_specs=[pl.BlockSpec((tm, K1), lambda i, j: (i, 0)),
                  pl.BlockSpec((tm, K2), lambda i, j: (i, 0)),
                  pl.BlockSpec((K1, tn), lambda i, j: (0, j)),
                  pl.BlockSpec((K2, tn), lambda i, j: (0, j))],
        out_specs=pl.BlockSpec((tm, tn), lambda i, j: (i, j)),
        out_shape=jax.ShapeDtypeStruct((M, N), BF16),
        compiler_params=_cparams("parallel", "parallel"),
        name="matmul_dual",
    )(a1, a2, w1, w2)


def _ffn(h, w_gate, w_up, w_down):
    act = _matmul_swiglu(h, w_gate, w_up, tm=1024, tn=512)
    kf = w_down.shape[0]
    return _matmul(act, w_down, BF16, tm=1024, tn=1024, tk=kf // 4, name="matmul_down")


def _diff_attn_kernel(lam_ref, q_ref, k_ref, v_ref, c_ref, sa_ref, sb_ref, w_ref, o_ref,
                      qs_ref, ks_ref, vs_ref, *, tq, lambda_init):
    S = q_ref.shape[0]
    c, sa, sb = c_ref[...], sa_ref[...], sb_ref[...]

    def rope(x):
        half = ROPE_DIM // 2
        return x * c + pltpu.roll(x, half, 1) * sa + pltpu.roll(x, LANES - half, 1) * sb

    q = rope(q_ref[...].astype(F32)) * (DIFF_QKDIM ** -0.5)
    k = rope(k_ref[...].astype(F32))
    first_map = lax.broadcasted_iota(jnp.int32, (S, LANES), 1) < DIFF_QKDIM
    qs_ref[0] = jnp.where(first_map, q, 0.0).astype(BF16)
    qs_ref[1] = jnp.where(first_map, 0.0, q).astype(BF16)
    ks_ref[...] = k.astype(BF16)
    vs_ref[...] = v_ref[...].astype(BF16)

    lp = lam_ref[...]
    lam = (jnp.exp(jnp.sum(lp[0:1] * lp[1:2], keepdims=True))
           - jnp.exp(jnp.sum(lp[2:3] * lp[3:4], keepdims=True)) + lambda_init)
    w = w_ref[...]

    def softmax_pv(qh):
        s = _dot_nt(qh, ks_ref[...])
        e = jnp.exp(s - jnp.max(s, axis=-1, keepdims=True))
        l = jnp.sum(e, axis=-1, keepdims=True)
        return _dot(e.astype(BF16), vs_ref[...]) / l

    def body(i, carry):
        r = pl.multiple_of(i * tq, tq)
        o = softmax_pv(qs_ref[0, pl.ds(r, tq), :]) - lam * softmax_pv(qs_ref[1, pl.ds(r, tq), :])
        o = _rms(o, w, SUBLN_EPS) * (1.0 - lambda_init)
        o_ref[pl.ds(r, tq), :] = o.astype(o_ref.dtype)
        return carry

    lax.fori_loop(0, S // tq, body, 0, unroll=2)


def _diff_attention(proj, rope_c, rope_sa, rope_sb, lam_params, subln_w, heads, lambda_init, tq=256):
    B, S, _ = proj.shape
    tq = min(tq, S)
    head = lambda off: pl.BlockSpec((None, S, LANES), lambda b, h: (b, 0, off + h))
    table = pl.BlockSpec((None, S, LANES), lambda b, h: (b, 0, 0))
    return pl.pallas_call(
        functools.partial(_diff_attn_kernel, tq=tq, lambda_init=lambda_init),
        grid=(B, heads),
        in_specs=[pl.BlockSpec((4, DIFF_QKDIM), lambda b, h: (0, 0)),
                  head(0), head(heads), head(2 * heads), table, table, table,
                  pl.BlockSpec((1, DIFF_VDIM), lambda b, h: (0, 0))],
        out_specs=pl.BlockSpec((None, S, LANES), lambda b, h: (b, 0, h)),
        out_shape=jax.ShapeDtypeStruct((B, S, heads * DIFF_VDIM), BF16),
        scratch_shapes=[pltpu.VMEM((2, S, LANES), BF16), pltpu.VMEM((S, LANES), BF16),
                        pltpu.VMEM((S, LANES), BF16)],
        compiler_params=_cparams("parallel", "parallel"),
        name="diff_attention",
    )(lam_params, proj, proj, proj, rope_c, rope_sa, rope_sb, subln_w.reshape(1, DIFF_VDIM))


def _bdot(a, b):
    return lax.dot_general(a, b, (((2,), (1,)), ((0,), (0,))), preferred_element_type=F32)


def _bdot_nt(a, b):
    return lax.dot_general(a, b, (((2,), (2,)), ((0,), (0,))), preferred_element_type=F32)


def _bdot_tn(a, b):
    return lax.dot_general(a, b, (((1,), (1,)), ((0,), (0,))), preferred_element_type=F32)


def _unit_tri_inverse(a, eye, same_block):
    mm = lambda x, y: _bdot(x.astype(BF16), y.astype(BF16))
    ad = jnp.where(same_block, a, 0.0)
    ao = a - ad
    p = eye - ad
    sq = ad
    for _ in range(int(math.log2(DN_BLOCK)) - 1):
        sq = mm(sq, sq)
        p = p + mm(p, sq)
    b = mm(p, ao)
    b2 = mm(b, b)
    x = eye - b + b2 - mm(b, b2)
    return mm(x, p)


def _gdn_kernel(alog_ref, dtb_ref, q_ref, k_ref, v_ref, z_ref, cq_ref, ck_ref, cv_ref,
                gcol_ref, grow_ref, nw_ref, o_ref,
                q16_ref, k16_ref, gc_ref, beta_ref, rhs_ref, kt_ref, egl_ref, gcr_ref,
                u_ref, lhs_ref, qkd_ref, qm_ref, vn_ref, od_ref):
    S = q_ref.shape[0]
    C = DN_CHUNK
    N = S // C
    h = pl.program_id(1)

    t = lax.broadcasted_iota(jnp.int32, (S, LANES), 0)
    pad = (DN_CONV - 1) // 2

    def conv_silu(x_ref, cw_ref):
        x = x_ref[...].astype(F32)
        cw = cw_ref[...].astype(F32)
        y = x * cw[pad:pad + 1]
        for i in range(DN_CONV):
            d = i - pad
            if d == 0:
                continue
            xs = pltpu.roll(x, (-d) % S, 0)
            valid = (t + d >= 0) & (t + d < S)
            y = y + jnp.where(valid, xs, 0.0) * cw[i:i + 1]
        return _silu(y)

    def l2n(x):
        return x * lax.rsqrt(jnp.sum(x * x, axis=-1, keepdims=True) + 1e-6)

    qn = l2n(conv_silu(q_ref, cq_ref)) * (DN_KDIM ** -0.5)
    kn = l2n(conv_silu(k_ref, ck_ref))
    vn = conv_silu(v_ref, cv_ref)
    q16_ref[...] = qn.astype(BF16)
    k16_ref[...] = kn.astype(BF16)

    ri = lax.broadcasted_iota(jnp.int32, (C, C), 0)
    ci = lax.broadcasted_iota(jnp.int32, (C, C), 1)
    eye = (ri == ci).astype(F32)
    same_block = (ri // DN_BLOCK) == (ci // DN_BLOCK)
    incl = (ri >= ci, ri <= ci)
    strict = (ri > ci, ri < ci)
    tri_row = (incl[1].astype(F32), incl[0].astype(F32))

    def log_decay(x, d):
        a = jnp.exp(jnp.full((1, 1), alog_ref[d, h], F32))
        xb = x + dtb_ref[d, h]
        return -a * (jnp.maximum(xb, 0.0) + jnp.log1p(jnp.exp(-jnp.abs(xb))))

    pos = t % C
    gl = gcol_ref[...].astype(F32)
    chunked = lambda m: m.reshape(N, C, LANES)
    for d in range(2):
        a_col = jnp.broadcast_to(gl[:, 2 * d:2 * d + 1], (S, LANES))
        b_col = jnp.broadcast_to(gl[:, 2 * d + 1:2 * d + 2], (S, LANES))
        beta = jax.nn.sigmoid(b_col)
        gc = log_decay(a_col, d)
        shift = 1
        while shift < C:
            if d == 0:
                gc = gc + jnp.where(pos >= shift, pltpu.roll(gc, shift, 0), 0.0)
            else:
                gc = gc + jnp.where(pos < C - shift, pltpu.roll(gc, S - shift, 0), 0.0)
            shift *= 2
        gc3 = chunked(gc)
        g_last = gc3[:, C - 1:C, :] if d == 0 else gc3[:, 0:1, :]
        egc = jnp.exp(gc)
        gc_ref[d] = gc
        beta_ref[d] = beta
        rhs_ref[d, :, :LANES] = (vn * beta).astype(BF16)
        rhs_ref[d, :, LANES:] = (kn * beta * egc).astype(BF16)
        kt_ref[d] = (chunked(kn) * jnp.exp(g_last - gc3)).reshape(S, LANES).astype(BF16)
        lhs_ref[d, :, DN_KDIM + C:, :] = chunked(qn * egc).astype(BF16)
        egl_ref[d] = jnp.broadcast_to(jnp.exp(g_last), (N, 8, LANES))
        g_row = log_decay(grow_ref[2 * d].astype(F32), d)
        gc_row = _dot_exact(g_row, tri_row[d])
        for n in range(N):
            gcr_ref[d, n] = jnp.broadcast_to(gc_row[n:n + 1, :], (8, C))

    U = DN_UNROLL

    def factor_body(i, carry):
        n0 = pl.multiple_of(i * U, U)
        rows = pl.ds(pl.multiple_of(i * (U * C), U * C), U * C)
        chunks = pl.ds(n0, U)
        per_chunk = lambda m: m.reshape(U, C, m.shape[-1])
        k16 = per_chunk(k16_ref[rows, :])
        kq = _bdot_nt(jnp.concatenate([k16, per_chunk(q16_ref[rows, :])], axis=1), k16)
        kk, qk = kq[:, :C], kq[:, C:]
        a = []
        for d in range(2):
            gcc = per_chunk(gc_ref[d, rows, :])[:, :, :C]
            diff = jnp.where(incl[d], gcc - gcr_ref[d, chunks][:, 0:1, :], 0.0)
            decay = jnp.where(incl[d], jnp.exp(diff), 0.0)
            qkd_ref[d, chunks] = (qk * decay).astype(BF16)
            a.append(jnp.where(strict[d], per_chunk(beta_ref[d, rows, :])[:, :, :C] * kk * decay, 0.0))
        tinv = _unit_tri_inverse(jnp.concatenate(a, axis=0), eye, same_block)
        rhs = jnp.concatenate([per_chunk(rhs_ref[d, rows, :]) for d in range(2)], axis=0)
        uw = _bdot(tinv.astype(BF16), rhs)
        uw16 = uw.astype(BF16)
        kt = jnp.concatenate([per_chunk(kt_ref[d, rows, :]) for d in range(2)], axis=0)
        pq = _bdot_tn(kt, uw16)
        for d in range(2):
            sel = slice(d * U, (d + 1) * U)
            u_ref[d, rows, :] = uw[sel, :, :LANES].reshape(U * C, LANES)
            qm_ref[d, chunks] = pq[sel, :, :LANES]
            lhs_ref[d, chunks, :DN_KDIM, :] = pq[sel, :, LANES:].astype(BF16)
            lhs_ref[d, chunks, DN_KDIM:DN_KDIM + C, :] = uw16[sel, :, LANES:]
        return carry

    lax.fori_loop(0, N // U, factor_body, 0)

    def scan_step(d, n, state):
        rows = pl.ds(pl.multiple_of(n * C, C), C)
        prod = _dot(lhs_ref[d, n], state.astype(BF16))
        vn_ref[d, rows, :] = (u_ref[d, rows, :] - prod[DN_KDIM:DN_KDIM + C]).astype(BF16)
        od_ref[d, rows, :] = prod[DN_KDIM + C:]
        return state * egl_ref[d, n][0:1, :] - prod[:DN_KDIM] + qm_ref[d, n]

    def scan_body(n, states):
        return scan_step(0, n, states[0]), scan_step(1, N - 1 - n, states[1])

    zero = jnp.zeros((DN_KDIM, DN_VDIM), F32)
    lax.fori_loop(0, N, scan_body, (zero, zero), unroll=2)

    def intra_body(i, carry):
        rows = pl.ds(pl.multiple_of(i * (U * C), U * C), U * C)
        chunks = pl.ds(pl.multiple_of(i * U, U), U)
        for d in range(2):
            intra = _bdot(qkd_ref[d, chunks], vn_ref[d, rows, :].reshape(U, C, LANES))
            od_ref[d, rows, :] += intra.reshape(U * C, LANES)
        return carry

    lax.fori_loop(0, N // U, intra_body, 0)

    o = od_ref[0] + od_ref[1]
    o = _rms(o, nw_ref[...], NORM_EPS) * _silu(z_ref[...].astype(F32))
    o_ref[...] = o.astype(o_ref.dtype)


def _gated_deltanet(proj, qkv_block0, z_block0, conv_w, gates_col, gates_row, a_log, dt_bias, norm_w, heads):
    B, S, _ = proj.shape
    N = S // DN_CHUNK
    head = lambda off: pl.BlockSpec((None, S, LANES), lambda b, h: (b, 0, off + h))
    cw = lambda off: pl.BlockSpec((DN_CONV, LANES), lambda b, h: (0, off + h))
    smem = pl.BlockSpec(memory_space=pltpu.SMEM)
    return pl.pallas_call(
        _gdn_kernel,
        grid=(B, heads),
        in_specs=[smem, smem,
                  head(qkv_block0), head(qkv_block0 + heads), head(qkv_block0 + 2 * heads), head(z_block0),
                  cw(0), cw(heads), cw(2 * heads),
                  pl.BlockSpec((None, None, S, 4), lambda b, h: (b, h, 0, 0)),
                  pl.BlockSpec((None, None, 4, N, DN_CHUNK), lambda b, h: (b, h, 0, 0, 0)),
                  pl.BlockSpec((1, DN_VDIM), lambda b, h: (0, 0))],
        out_specs=pl.BlockSpec((None, S, LANES), lambda b, h: (b, 0, h)),
        out_shape=jax.ShapeDtypeStruct((B, S, heads * DN_VDIM), BF16),
        scratch_shapes=[
            pltpu.VMEM((S, LANES), BF16),
            pltpu.VMEM((S, LANES), BF16),
            pltpu.VMEM((2, S, LANES), F32),
            pltpu.VMEM((2, S, LANES), F32),
            pltpu.VMEM((2, S, 2 * LANES), BF16),
            pltpu.VMEM((2, S, LANES), BF16),
            pltpu.VMEM((2, N, 8, LANES), F32),
            pltpu.VMEM((2, N, 8, DN_CHUNK), F32),
            pltpu.VMEM((2, S, LANES), F32),
            pltpu.VMEM((2, N, DN_KDIM + 2 * DN_CHUNK, LANES), BF16),
            pltpu.VMEM((2, N, DN_CHUNK, DN_CHUNK), BF16),
            pltpu.VMEM((2, N, DN_KDIM, LANES), F32),
            pltpu.VMEM((2, S, LANES), BF16),
            pltpu.VMEM((2, S, LANES), F32),
        ],
        compiler_params=_cparams("parallel", "parallel"),
        name="gated_deltanet",
    )(a_log, dt_bias, proj, proj, proj, proj, conv_w, conv_w, conv_w, gates_col, gates_row,
      norm_w.reshape(1, DN_VDIM))


def _cross_attn_kernel(q_ref, kv_ref, o_ref):
    width = MEM_HEADS * MEM_HDIM
    outs = []
    for hd in range(MEM_HEADS):
        cols = slice(hd * MEM_HDIM, (hd + 1) * MEM_HDIM)
        q = q_ref[:, cols]
        k = kv_ref[:, cols]
        v = kv_ref[:, width + hd * MEM_HDIM:width + (hd + 1) * MEM_HDIM]
        s = _dot_nt(q, k) * (MEM_HDIM ** -0.5)
        e = jnp.exp(s - jnp.max(s, axis=-1, keepdims=True))
        l = jnp.sum(e, axis=-1, keepdims=True)
        outs.append(_dot(e.astype(BF16), v) / l)
    o_ref[...] = jnp.concatenate(outs, axis=1).astype(o_ref.dtype)


def _cross_attention(q, kv, tq=512):
    B, S, W = q.shape
    M = kv.shape[1]
    tq = min(tq, S)
    return pl.pallas_call(
        _cross_attn_kernel,
        grid=(B, S // tq),
        in_specs=[pl.BlockSpec((None, tq, W), lambda b, i: (b, i, 0)),
                  pl.BlockSpec((None, M, 2 * W), lambda b, i: (b, 0, 0))],
        out_specs=pl.BlockSpec((None, tq, W), lambda b, i: (b, i, 0)),
        out_shape=jax.ShapeDtypeStruct((B, S, W), BF16),
        compiler_params=_cparams("parallel", "parallel"),
        name="cross_attention",
    )(q, kv)


def _rope_tables(positions):
    B, S = positions.shape
    half = ROPE_DIM // 2
    inv_freq = ROPE_THETA ** (-jnp.arange(0, ROPE_DIM, 2, dtype=F32) / ROPE_DIM)
    ang = positions.astype(F32)[..., None] * inv_freq
    cos, sin = jnp.cos(ang), jnp.sin(ang)
    rest = DIFF_QKDIM - ROPE_DIM
    ones, zeros = jnp.ones((B, S, rest), F32), jnp.zeros((B, S, rest), F32)
    z8 = jnp.zeros((B, S, half), F32)
    both = lambda m: jnp.concatenate([m, m], axis=-1)
    c = both(jnp.concatenate([cos, cos, ones], axis=-1))
    sa = both(jnp.concatenate([z8, sin, zeros], axis=-1))
    sb = both(jnp.concatenate([-sin, z8, zeros], axis=-1))
    return c, sa, sb


def _pad_cols(w, n):
    return jnp.pad(w, ((0, 0), (0, n - w.shape[1])))


def _ffn_weights(w_gu, w_down):
    d_ff = w_down.shape[0]
    d_pad = -(-d_ff // FF_ALIGN) * FF_ALIGN
    w_gu = w_gu.astype(BF16)
    w_up = _pad_cols(w_gu[:, d_ff:], d_pad)
    w_dn = jnp.pad(w_down.astype(BF16), ((0, d_pad - d_ff), (0, 0)))
    return w_gu, w_up, w_dn


def kernel(x, mem, positions, ffn1_norms, ffn1_w_gu, ffn1_w_down, mix_norms, mix_w_in, dn_conv_w, dn_a_log, dn_dt_bias, dn_norm_w, diff_lambda, diff_subln_w, mix_w_out, mem_norms, mem_w_q, mem_w_kv, mem_w_o, ffn2_norms, ffn2_w_gu, ffn2_w_down):
    B, S, D = x.shape
    T = B * S
    depth = ffn1_norms.shape[0]
    diff_width = D // 2
    dn_width = D - diff_width
    diff_heads = diff_width // DIFF_VDIM
    dn_heads = dn_width // DN_VDIM
    main_cols = 3 * diff_width + 4 * dn_width
    n_chunks = S // DN_CHUNK

    rope_c, rope_sa, rope_sb = _rope_tables(positions)
    xt = x.reshape(T, D)
    h = _norm_cast(xt, ffn1_norms[0, 0])
    for l in range(depth):
        lambda_init = 0.8 - 0.6 * math.exp(-0.3 * l)
        y = _ffn(h, *_ffn_weights(ffn1_w_gu[l], ffn1_w_down[l]))
        xt, h = _post_pre(y, xt, ffn1_norms[l, 1], mix_norms[l, 0], 0.5)
        w_in = mix_w_in[l].astype(BF16)
        proj = _matmul(h, w_in, BF16, tm=1024, tn=1024, n_cols=main_cols, name="matmul_in")
        gates = _matmul(h, _pad_cols(w_in[:, main_cols:], LANES), F32, tm=1024, tn=LANES,
                        name="matmul_gates")
        proj = proj.reshape(B, S, main_cols)
        gates = gates.reshape(B, S, LANES)[:, :, :4 * dn_heads].reshape(B, S, 4, dn_heads)
        gates_col = gates.transpose(0, 3, 1, 2)
        gates_row = gates.transpose(0, 3, 2, 1).reshape(B, dn_heads, 4, n_chunks, DN_CHUNK)
        o_diff = _diff_attention(proj, rope_c, rope_sa, rope_sb, diff_lambda[l], diff_subln_w[l],
                                 diff_heads, lambda_init)
        o_dn = _gated_deltanet(proj, 3 * diff_heads, 3 * diff_heads + 3 * dn_heads, dn_conv_w[l],
                               gates_col, gates_row, dn_a_log[l], dn_dt_bias[l], dn_norm_w[l], dn_heads)
        w_out = mix_w_out[l]
        mixed = _matmul_dual(o_diff.reshape(T, diff_width), o_dn.reshape(T, dn_width),
                             w_out[:diff_width].astype(BF16), w_out[diff_width:].astype(BF16),
                             tm=1024, tn=1024)
        xt, h = _post_pre(mixed, xt, mix_norms[l, 1], mem_norms[l, 0], 1.0)
        mlen = mem.shape[1]
        memn = _norm_cast(mem.reshape(B * mlen, D), mem_norms[l, 1])
        kv = _matmul(memn, mem_w_kv[l].astype(BF16), BF16, tm=1024, tn=1024, name="matmul_mem_kv")
        q = _matmul(h, mem_w_q[l].astype(BF16), BF16, tm=1024, tn=512, name="matmul_mem_q")
        width = MEM_HEADS * MEM_HDIM
        o_mem = _cross_attention(q.reshape(B, S, width), kv.reshape(B, mlen, 2 * width))
        c = _matmul(o_mem.reshape(T, width), mem_w_o[l].astype(BF16), BF16, tm=1024, tn=1024,
                    name="matmul_mem_o")
        xt, h = _post_pre(c, xt, mem_norms[l, 2], ffn2_norms[l, 0], 1.0)
        y = _ffn(h, *_ffn_weights(ffn2_w_gu[l], ffn2_w_down[l]))
        if l + 1 < depth:
            xt, h = _post_pre(y, xt, ffn2_norms[l, 1], ffn1_norms[l + 1, 0], 0.5)
        else:
            xt = _post_pre(y, xt, ffn2_norms[l, 1], None, 0.5)
    return xt.reshape(B, S, D)
```

```python
import functools
import math

import jax
import jax.numpy as jnp
from jax import lax
from jax.experimental import pallas as pl
from jax.experimental.pallas import tpu as pltpu

F32 = jnp.float32
BF16 = jnp.bfloat16

DIFF_VDIM = 128
DIFF_QKDIM = 64
DN_KDIM = 128
DN_VDIM = 128
DN_CONV = 5
DN_CHUNK = 64
DN_BLOCK = 16
DN_UNROLL = 32
ROPE_THETA = 500000.0
ROPE_DIM = DIFF_QKDIM // 4
MEM_HEADS = 4
MEM_HDIM = 128
NORM_EPS = 1e-6
SUBLN_EPS = 1e-5
LANES = 128
GATE_REP = LANES // 4
FF_ALIGN = 1024
VMEM_LIMIT = 56 * 1024 * 1024


def _cparams(*sem):
    return pltpu.CompilerParams(dimension_semantics=sem, vmem_limit_bytes=VMEM_LIMIT)


def _dot(a, b):
    return jnp.dot(a, b, preferred_element_type=F32)


def _dot_nt(a, b):
    return lax.dot_general(a, b, (((1,), (1,)), ((), ())), preferred_element_type=F32)


def _dot_tn(a, b):
    return lax.dot_general(a, b, (((0,), (0,)), ((), ())), preferred_element_type=F32)


def _dot_exact(a, b):
    return jnp.dot(a, b, preferred_element_type=F32, precision=lax.Precision.HIGHEST)


def _rms(x, w, eps):
    return x * lax.rsqrt(jnp.mean(x * x, axis=-1, keepdims=True) + eps) * w


def _silu(x):
    return x * jax.nn.sigmoid(x)


def _norm_kernel(x_ref, w_ref, o_ref):
    o_ref[...] = _rms(x_ref[...], w_ref[...], NORM_EPS).astype(o_ref.dtype)


def _norm_cast(x, w, tm=256):
    T, D = x.shape
    return pl.pallas_call(
        _norm_kernel,
        grid=(T // tm,),
        in_specs=[pl.BlockSpec((tm, D), lambda i: (i, 0)),
                  pl.BlockSpec((1, D), lambda i: (0, 0))],
        out_specs=pl.BlockSpec((tm, D), lambda i: (i, 0)),
        out_shape=jax.ShapeDtypeStruct((T, D), BF16),
        compiler_params=_cparams("parallel"),
        name="norm_cast",
    )(x, w.reshape(1, D))


def _post_pre_kernel(y_ref, x_ref, wpost_ref, wpre_ref, xo_ref, h_ref, *, scale):
    xn = x_ref[...] + scale * _rms(y_ref[...].astype(F32), wpost_ref[...], NORM_EPS)
    xo_ref[...] = xn
    h_ref[...] = _rms(xn, wpre_ref[...], NORM_EPS).astype(h_ref.dtype)


def _post_kernel(y_ref, x_ref, wpost_ref, xo_ref, *, scale):
    xo_ref[...] = x_ref[...] + scale * _rms(y_ref[...].astype(F32), wpost_ref[...], NORM_EPS)


def _post_pre(y, x, w_post, w_pre, scale, tm=256):
    T, D = x.shape
    row = pl.BlockSpec((tm, D), lambda i: (i, 0))
    vec = pl.BlockSpec((1, D), lambda i: (0, 0))
    if w_pre is None:
        return pl.pallas_call(
            functools.partial(_post_kernel, scale=scale),
            grid=(T // tm,),
            in_specs=[row, row, vec],
            out_specs=row,
            out_shape=jax.ShapeDtypeStruct((T, D), F32),
            compiler_params=_cparams("parallel"),
            name="post_norm_residual",
        )(y, x, w_post.reshape(1, D))
    return pl.pallas_call(
        functools.partial(_post_pre_kernel, scale=scale),
        grid=(T // tm,),
        in_specs=[row, row, vec, vec],
        out_specs=[row, row],
        out_shape=[jax.ShapeDtypeStruct((T, D), F32), jax.ShapeDtypeStruct((T, D), BF16)],
        compiler_params=_cparams("parallel"),
        name="post_norm_residual_pre_norm",
    )(y, x, w_post.reshape(1, D), w_pre.reshape(1, D))


def _mm_kernel(a_ref, w_ref, o_ref, *acc, nk):
    if nk == 1:
        o_ref[...] = _dot(a_ref[...], w_ref[...]).astype(o_ref.dtype)
        return
    acc_ref, = acc
    k = pl.program_id(2)

    @pl.when(k == 0)
    def _():
        acc_ref[...] = jnp.zeros_like(acc_ref)

    acc_ref[...] += _dot(a_ref[...], w_ref[...])

    @pl.when(k == nk - 1)
    def _():
        o_ref[...] = acc_ref[...].astype(o_ref.dtype)


def _matmul(a, w, out_dtype, tm, tn, tk=None, n_cols=None, name="matmul"):
    M, K = a.shape
    N = w.shape[1] if n_cols is None else n_cols
    tk = K if tk is None else tk
    tm, tn = min(tm, M), min(tn, N)
    nk = K // tk
    assert M % tm == 0 and N % tn == 0 and K % tk == 0
    return pl.pallas_call(
        functools.partial(_mm_kernel, nk=nk),
        grid=(M // tm, N // tn, nk),
        in_specs=[pl.BlockSpec((tm, tk), lambda i, j, k: (i, k)),
                  pl.BlockSpec((tk, tn), lambda i, j, k: (k, j))],
        out_specs=pl.BlockSpec((tm, tn), lambda i, j, k: (i, j)),
        out_shape=jax.ShapeDtypeStruct((M, N), out_dtype),
        scratch_shapes=[pltpu.VMEM((tm, tn), F32)] if nk > 1 else [],
        compiler_params=_cparams("parallel", "parallel", "arbitrary"),
        name=name,
    )(a, w)


def _mm_swiglu_kernel(a_ref, wg_ref, wu_ref, o_ref):
    a = a_ref[...]
    g = _dot(a, wg_ref[...])
    u = _dot(a, wu_ref[...])
    o_ref[...] = (_silu(g) * u).astype(o_ref.dtype)


def _matmul_swiglu(a, w_gate, w_up, tm, tn):
    M, K = a.shape
    N = w_up.shape[1]
    tm = min(tm, M)
    assert M % tm == 0 and N % tn == 0 and w_gate.shape[1] >= N
    wspec = pl.BlockSpec((K, tn), lambda i, j: (0, j))
    return pl.pallas_call(
        _mm_swiglu_kernel,
        grid=(M // tm, N // tn),
        in_specs=[pl.BlockSpec((tm, K), lambda i, j: (i, 0)), wspec, wspec],
        out_specs=pl.BlockSpec((tm, tn), lambda i, j: (i, j)),
        out_shape=jax.ShapeDtypeStruct((M, N), BF16),
        compiler_params=_cparams("parallel", "parallel"),
        name="matmul_swiglu",
    )(a, w_gate, w_up)


def _mm_dual_kernel(a1_ref, a2_ref, w1_ref, w2_ref, o_ref):
    o_ref[...] = (_dot(a1_ref[...], w1_ref[...]) + _dot(a2_ref[...], w2_ref[...])).astype(o_ref.dtype)


def _matmul_dual(a1, a2, w1, w2, tm, tn):
    M, K1 = a1.shape
    K2 = a2.shape[1]
    N = w1.shape[1]
    tm = min(tm, M)
    assert M % tm == 0 and N % tn == 0
    return pl.pallas_call(
        _mm_dual_kernel,
        grid=(M // tm, N // tn),
        in_specs=[pl.BlockSpec((tm, K1), lambda i, j: (i, 0)),
                  pl.BlockSpec((tm, K2), lambda i, j: (i, 0)),
                  pl.BlockSpec((K1, tn), lambda i, j: (0, j)),
                  pl.BlockSpec((K2, tn), lambda i, j: (0, j))],
        out_specs=pl.BlockSpec((tm, tn), lambda i, j: (i, j)),
        out_shape=jax.ShapeDtypeStruct((M, N), BF16),
        compiler_params=_cparams("parallel", "parallel"),
        name="matmul_dual",
    )(a1, a2, w1, w2)


def _ffn(h, w_gate, w_up, w_down):
    act = _matmul_swiglu(h, w_gate, w_up, tm=1024, tn=512)
    kf = w_down.shape[0]
    return _matmul(act, w_down, BF16, tm=1024, tn=1024, tk=kf // 4, name="matmul_down")


def _diff_attn_kernel(lam_ref, q_ref, k_ref, v_ref, c_ref, sa_ref, sb_ref, w_ref, o_ref,
                      qs_ref, ks_ref, vs_ref, *, tq, lambda_init):
    S = q_ref.shape[0]
    c, sa, sb = c_ref[...], sa_ref[...], sb_ref[...]

    def rope(x):
        half = ROPE_DIM // 2
        return x * c + pltpu.roll(x, half, 1) * sa + pltpu.roll(x, LANES - half, 1) * sb

    q = rope(q_ref[...].astype(F32)) * (DIFF_QKDIM ** -0.5)
    k = rope(k_ref[...].astype(F32))
    first_map = lax.broadcasted_iota(jnp.int32, (S, LANES), 1) < DIFF_QKDIM
    qs_ref[0] = jnp.where(first_map, q, 0.0).astype(BF16)
    qs_ref[1] = jnp.where(first_map, 0.0, q).astype(BF16)
    ks_ref[...] = k.astype(BF16)
    vs_ref[...] = v_ref[...].astype(BF16)

    lp = lam_ref[...]
    lam = (jnp.exp(jnp.sum(lp[0:1] * lp[1:2], keepdims=True))
           - jnp.exp(jnp.sum(lp[2:3] * lp[3:4], keepdims=True)) + lambda_init)
    w = w_ref[...]

    def softmax_pv(s):
        e = jnp.exp(s - jnp.max(s, axis=-1, keepdims=True))
        l = jnp.sum(e, axis=-1, keepdims=True)
        return _dot(e.astype(BF16), vs_ref[...]) / l

    def scores(m, i):
        return _dot_nt(qs_ref[m, i * tq:(i + 1) * tq, :], ks_ref[...])

    nq = S // tq
    s1, s2 = scores(0, 0), scores(1, 0)
    for i in range(nq):
        last = i + 1 == nq
        s1_next = None if last else scores(0, i + 1)
        o1 = softmax_pv(s1)
        s2_next = None if last else scores(1, i + 1)
        o = o1 - lam * softmax_pv(s2)
        o = _rms(o, w, SUBLN_EPS) * (1.0 - lambda_init)
        o_ref[i * tq:(i + 1) * tq, :] = o.astype(o_ref.dtype)
        s1, s2 = s1_next, s2_next


def _diff_attention(proj, rope_c, rope_sa, rope_sb, lam_params, subln_w, heads, lambda_init, tq=512):
    B, S, _ = proj.shape
    tq = min(tq, S)
    head = lambda off: pl.BlockSpec((None, S, LANES), lambda b, h: (b, 0, off + h))
    table = pl.BlockSpec((None, S, LANES), lambda b, h: (b, 0, 0))
    return pl.pallas_call(
        functools.partial(_diff_attn_kernel, tq=tq, lambda_init=lambda_init),
        grid=(B, heads),
        in_specs=[pl.BlockSpec((4, DIFF_QKDIM), lambda b, h: (0, 0)),
                  head(0), head(heads), head(2 * heads), table, table, table,
                  pl.BlockSpec((1, DIFF_VDIM), lambda b, h: (0, 0))],
        out_specs=pl.BlockSpec((None, S, LANES), lambda b, h: (b, 0, h)),
        out_shape=jax.ShapeDtypeStruct((B, S, heads * DIFF_VDIM), BF16),
        scratch_shapes=[pltpu.VMEM((2, S, LANES), BF16), pltpu.VMEM((S, LANES), BF16),
                        pltpu.VMEM((S, LANES), BF16)],
        compiler_params=_cparams("parallel", "parallel"),
        name="diff_attention",
    )(lam_params, proj, proj, proj, rope_c, rope_sa, rope_sb, subln_w.reshape(1, DIFF_VDIM))


def _bdot(a, b):
    return lax.dot_general(a, b, (((2,), (1,)), ((0,), (0,))), preferred_element_type=F32)


def _bdot_nt(a, b):
    return lax.dot_general(a, b, (((2,), (2,)), ((0,), (0,))), preferred_element_type=F32)


def _bdot_tn(a, b):
    return lax.dot_general(a, b, (((1,), (1,)), ((0,), (0,))), preferred_element_type=F32)


def _unit_tri_inverse(a, eye, same_block):
    mm = lambda x, y: _bdot(x.astype(BF16), y.astype(BF16))
    ad = jnp.where(same_block, a, 0.0)
    ao = a - ad
    p = eye - ad
    sq = ad
    for _ in range(int(math.log2(DN_BLOCK)) - 1):
        sq = mm(sq, sq)
        p = p + mm(p, sq)
    b = mm(p, ao)
    b2 = mm(b, b)
    x = eye - b + b2 - mm(b, b2)
    return mm(x, p)


def _gdn_kernel(alog_ref, dtb_ref, q_ref, k_ref, v_ref, z_ref, cq_ref, ck_ref, cv_ref,
                gcol_ref, grow_ref, nw_ref, o_ref,
                q16_ref, k16_ref, gc_ref, beta_ref, rhs_ref, kt_ref, egl_ref, gcr_ref,
                u_ref, lhs_ref, qkd_ref, qm_ref, vn_ref, od_ref):
    S = q_ref.shape[0]
    C = DN_CHUNK
    N = S // C
    h = pl.program_id(1)

    t = lax.broadcasted_iota(jnp.int32, (S, LANES), 0)
    pad = (DN_CONV - 1) // 2

    def conv_silu(x_ref, cw_ref):
        x = x_ref[...].astype(F32)
        cw = cw_ref[...].astype(F32)
        y = x * cw[pad:pad + 1]
        for i in range(DN_CONV):
            d = i - pad
            if d == 0:
                continue
            xs = pltpu.roll(x, (-d) % S, 0)
            valid = (t + d >= 0) & (t + d < S)
            y = y + jnp.where(valid, xs, 0.0) * cw[i:i + 1]
        return _silu(y)

    def l2n(x):
        return x * lax.rsqrt(jnp.sum(x * x, axis=-1, keepdims=True) + 1e-6)

    qn = l2n(conv_silu(q_ref, cq_ref)) * (DN_KDIM ** -0.5)
    kn = l2n(conv_silu(k_ref, ck_ref))
    vn = conv_silu(v_ref, cv_ref)
    q16_ref[...] = qn.astype(BF16)
    k16_ref[...] = kn.astype(BF16)

    ri = lax.broadcasted_iota(jnp.int32, (C, C), 0)
    ci = lax.broadcasted_iota(jnp.int32, (C, C), 1)
    eye = (ri == ci).astype(F32)
    same_block = (ri // DN_BLOCK) == (ci // DN_BLOCK)
    incl = (ri >= ci, ri <= ci)
    strict = (ri > ci, ri < ci)
    tri_row = (incl[1].astype(F32), incl[0].astype(F32))

    def softplus(x):
        return jnp.maximum(x, 0.0) + jnp.log1p(jnp.exp(-jnp.abs(x)))

    def log_decay(x, d):
        a = jnp.exp(jnp.full((1, 1), alog_ref[d, h], F32))
        return -a * softplus(x + dtb_ref[d, h])

    pos = t % C
    grp = lax.broadcasted_iota(jnp.int32, (1, LANES), 1) // GATE_REP
    per_dir = lambda ref: jnp.where(grp == 0, ref[0, h], jnp.where(grp == 2, ref[1, h], 0.0))
    chunked = lambda m: m.reshape(N, C, LANES)
    gx = gcol_ref[...].astype(F32)
    g = -jnp.exp(per_dir(alog_ref)) * softplus(gx + per_dir(dtb_ref))
    beta_x = jax.nn.sigmoid(gx)
    pre = g
    shift = 1
    while shift < C:
        pre = pre + jnp.where(pos >= shift, pltpu.roll(pre, shift, 0), 0.0)
        shift *= 2
    pre3 = chunked(pre)
    tot = pre3[:, C - 1:C, :]
    gc_x = jnp.where(grp == 2, tot - pre3 + chunked(g), pre3).reshape(S, LANES)
    from_group = lambda m, q: jnp.broadcast_to(m[:, q * GATE_REP:q * GATE_REP + 1], (S, LANES))
    for d in range(2):
        beta = from_group(beta_x, 2 * d + 1)
        gc = from_group(gc_x, 2 * d)
        gc3 = chunked(gc)
        g_last = gc3[:, C - 1:C, :] if d == 0 else gc3[:, 0:1, :]
        egc = jnp.exp(gc)
        gc_ref[d] = gc
        beta_ref[d] = beta
        rhs_ref[d, :, :LANES] = (vn * beta).astype(BF16)
        rhs_ref[d, :, LANES:] = (kn * beta * egc).astype(BF16)
        kt_ref[d] = (chunked(kn) * jnp.exp(g_last - gc3)).reshape(S, LANES).astype(BF16)
        lhs_ref[d, :, DN_KDIM + C:, :] = chunked(qn * egc).astype(BF16)
        egl_ref[d] = jnp.broadcast_to(jnp.exp(g_last), (N, 8, LANES))
        g_row = log_decay(grow_ref[2 * d].astype(F32), d)
        gc_row = _dot_exact(g_row, tri_row[d])
        for n in range(N):
            gcr_ref[d, n] = jnp.broadcast_to(gc_row[n:n + 1, :], (8, C))

    U = min(DN_UNROLL, N)

    def factor_body(i, carry):
        n0 = pl.multiple_of(i * U, U)
        rows = pl.ds(pl.multiple_of(i * (U * C), U * C), U * C)
        chunks = pl.ds(n0, U)
        per_chunk = lambda m: m.reshape(U, C, m.shape[-1])
        k16 = per_chunk(k16_ref[rows, :])
        kq = _bdot_nt(jnp.concatenate([k16, per_chunk(q16_ref[rows, :])], axis=1), k16)
        kk, qk = kq[:, :C], kq[:, C:]
        a = []
        for d in range(2):
            gcc = per_chunk(gc_ref[d, rows, :])[:, :, :C]
            diff = jnp.where(incl[d], gcc - gcr_ref[d, chunks][:, 0:1, :], 0.0)
            decay = jnp.where(incl[d], jnp.exp(diff), 0.0)
            qkd_ref[d, chunks] = (qk * decay).astype(BF16)
            a.append(jnp.where(strict[d], per_chunk(beta_ref[d, rows, :])[:, :, :C] * kk * decay, 0.0))
        tinv = _unit_tri_inverse(jnp.concatenate(a, axis=0), eye, same_block)
        rhs = jnp.concatenate([per_chunk(rhs_ref[d, rows, :]) for d in range(2)], axis=0)
        uw = _bdot(tinv.astype(BF16), rhs)
        uw16 = uw.astype(BF16)
        kt = jnp.concatenate([per_chunk(kt_ref[d, rows, :]) for d in range(2)], axis=0)
        pq = _bdot_tn(kt, uw16)
        for d in range(2):
            sel = slice(d * U, (d + 1) * U)
            u_ref[d, rows, :] = uw[sel, :, :LANES].reshape(U * C, LANES)
            qm_ref[d, chunks] = pq[sel, :, :LANES]
            lhs_ref[d, chunks, :DN_KDIM, :] = pq[sel, :, LANES:].astype(BF16)
            lhs_ref[d, chunks, DN_KDIM:DN_KDIM + C, :] = uw16[sel, :, LANES:]
        return carry

    lax.fori_loop(0, N // U, factor_body, 0)

    def scan_step(d, n, state):
        rows = pl.ds(pl.multiple_of(n * C, C), C)
        prod = _dot(lhs_ref[d, n], state.astype(BF16))
        vn_ref[d, rows, :] = (u_ref[d, rows, :] - prod[DN_KDIM:DN_KDIM + C]).astype(BF16)
        od_ref[d, rows, :] = prod[DN_KDIM + C:]
        return state * egl_ref[d, n][0:1, :] - prod[:DN_KDIM] + qm_ref[d, n]

    def scan_body(n, states):
        return scan_step(0, n, states[0]), scan_step(1, N - 1 - n, states[1])

    zero = jnp.zeros((DN_KDIM, DN_VDIM), F32)
    lax.fori_loop(0, N, scan_body, (zero, zero), unroll=2)

    def intra_body(i, carry):
        rows = pl.ds(pl.multiple_of(i * (U * C), U * C), U * C)
        chunks = pl.ds(pl.multiple_of(i * U, U), U)
        for d in range(2):
            intra = _bdot(qkd_ref[d, chunks], vn_ref[d, rows, :].reshape(U, C, LANES))
            od_ref[d, rows, :] += intra.reshape(U * C, LANES)
        return carry

    lax.fori_loop(0, N // U, intra_body, 0)

    o = od_ref[0] + od_ref[1]
    o = _rms(o, nw_ref[...], NORM_EPS) * _silu(z_ref[...].astype(F32))
    o_ref[...] = o.astype(o_ref.dtype)


def _gated_deltanet(proj, qkv_block0, z_block0, conv_w, gates_col, gates_row, a_log, dt_bias, norm_w, heads):
    B, S, _ = proj.shape
    N = S // DN_CHUNK
    head = lambda off: pl.BlockSpec((None, S, LANES), lambda b, h: (b, 0, off + h))
    cw = lambda off: pl.BlockSpec((DN_CONV, LANES), lambda b, h: (0, off + h))
    smem = pl.BlockSpec(memory_space=pltpu.SMEM)
    return pl.pallas_call(
        _gdn_kernel,
        grid=(B, heads),
        in_specs=[smem, smem,
                  head(qkv_block0), head(qkv_block0 + heads), head(qkv_block0 + 2 * heads), head(z_block0),
                  cw(0), cw(heads), cw(2 * heads),
                  pl.BlockSpec((None, None, S, LANES), lambda b, h: (b, h, 0, 0)),
                  pl.BlockSpec((None, None, 4, N, DN_CHUNK), lambda b, h: (b, h, 0, 0, 0)),
                  pl.BlockSpec((1, DN_VDIM), lambda b, h: (0, 0))],
        out_specs=pl.BlockSpec((None, S, LANES), lambda b, h: (b, 0, h)),
        out_shape=jax.ShapeDtypeStruct((B, S, heads * DN_VDIM), BF16),
        scratch_shapes=[
            pltpu.VMEM((S, LANES), BF16),
            pltpu.VMEM((S, LANES), BF16),
            pltpu.VMEM((2, S, LANES), F32),
            pltpu.VMEM((2, S, LANES), F32),
            pltpu.VMEM((2, S, 2 * LANES), BF16),
            pltpu.VMEM((2, S, LANES), BF16),
            pltpu.VMEM((2, N, 8, LANES), F32),
            pltpu.VMEM((2, N, 8, DN_CHUNK), F32),
            pltpu.VMEM((2, S, LANES), F32),
            pltpu.VMEM((2, N, DN_KDIM + 2 * DN_CHUNK, LANES), BF16),
            pltpu.VMEM((2, N, DN_CHUNK, DN_CHUNK), BF16),
            pltpu.VMEM((2, N, DN_KDIM, LANES), F32),
            pltpu.VMEM((2, S, LANES), BF16),
            pltpu.VMEM((2, S, LANES), F32),
        ],
        compiler_params=_cparams("parallel", "parallel"),
        name="gated_deltanet",
    )(a_log, dt_bias, proj, proj, proj, proj, conv_w, conv_w, conv_w, gates_col, gates_row,
      norm_w.reshape(1, DN_VDIM))


def _cross_attn_kernel(q_ref, kv_ref, o_ref):
    width = MEM_HEADS * MEM_HDIM
    outs = []
    for hd in range(MEM_HEADS):
        cols = slice(hd * MEM_HDIM, (hd + 1) * MEM_HDIM)
        q = q_ref[:, cols]
        k = kv_ref[:, cols]
        v = kv_ref[:, width + hd * MEM_HDIM:width + (hd + 1) * MEM_HDIM]
        s = _dot_nt(q, k) * (MEM_HDIM ** -0.5)
        e = jnp.exp(s - jnp.max(s, axis=-1, keepdims=True))
        l = jnp.sum(e, axis=-1, keepdims=True)
        outs.append(_dot(e.astype(BF16), v) / l)
    o_ref[...] = jnp.concatenate(outs, axis=1).astype(o_ref.dtype)


def _cross_attention(q, kv, tq=512):
    B, S, W = q.shape
    M = kv.shape[1]
    tq = min(tq, S)
    return pl.pallas_call(
        _cross_attn_kernel,
        grid=(B, S // tq),
        in_specs=[pl.BlockSpec((None, tq, W), lambda b, i: (b, i, 0)),
                  pl.BlockSpec((None, M, 2 * W), lambda b, i: (b, 0, 0))],
        out_specs=pl.BlockSpec((None, tq, W), lambda b, i: (b, i, 0)),
        out_shape=jax.ShapeDtypeStruct((B, S, W), BF16),
        compiler_params=_cparams("parallel", "parallel"),
        name="cross_attention",
    )(q, kv)


def _rope_tables(positions):
    B, S = positions.shape
    half = ROPE_DIM // 2
    inv_freq = ROPE_THETA ** (-jnp.arange(0, ROPE_DIM, 2, dtype=F32) / ROPE_DIM)
    ang = positions.astype(F32)[..., None] * inv_freq
    cos, sin = jnp.cos(ang), jnp.sin(ang)
    rest = DIFF_QKDIM - ROPE_DIM
    ones, zeros = jnp.ones((B, S, rest), F32), jnp.zeros((B, S, rest), F32)
    z8 = jnp.zeros((B, S, half), F32)
    both = lambda m: jnp.concatenate([m, m], axis=-1)
    c = both(jnp.concatenate([cos, cos, ones], axis=-1))
    sa = both(jnp.concatenate([z8, sin, zeros], axis=-1))
    sb = both(jnp.concatenate([-sin, z8, zeros], axis=-1))
    return c, sa, sb


def _pad_cols(w, n):
    return jnp.pad(w, ((0, 0), (0, n - w.shape[1])))


def _ffn_weights(w_gu, w_down):
    d_ff = w_down.shape[0]
    d_pad = -(-d_ff // FF_ALIGN) * FF_ALIGN
    w_gu = w_gu.astype(BF16)
    w_up = _pad_cols(w_gu[:, d_ff:], d_pad)
    w_dn = jnp.pad(w_down.astype(BF16), ((0, d_pad - d_ff), (0, 0)))
    return w_gu, w_up, w_dn


def kernel(x, mem, positions, ffn1_norms, ffn1_w_gu, ffn1_w_down, mix_norms, mix_w_in, dn_conv_w, dn_a_log, dn_dt_bias, dn_norm_w, diff_lambda, diff_subln_w, mix_w_out, mem_norms, mem_w_q, mem_w_kv, mem_w_o, ffn2_norms, ffn2_w_gu, ffn2_w_down):
    B, S, D = x.shape
    T = B * S
    depth = ffn1_norms.shape[0]
    diff_width = D // 2
    dn_width = D - diff_width
    diff_heads = diff_width // DIFF_VDIM
    dn_heads = dn_width // DN_VDIM
    main_cols = 3 * diff_width + 4 * dn_width
    n_chunks = S // DN_CHUNK

    rope_c, rope_sa, rope_sb = _rope_tables(positions)
    xt = x.reshape(T, D)
    h = _norm_cast(xt, ffn1_norms[0, 0])
    for l in range(depth):
        lambda_init = 0.8 - 0.6 * math.exp(-0.3 * l)
        y = _ffn(h, *_ffn_weights(ffn1_w_gu[l], ffn1_w_down[l]))
        xt, h = _post_pre(y, xt, ffn1_norms[l, 1], mix_norms[l, 0], 0.5)
        w_in = mix_w_in[l].astype(BF16)
        proj = _matmul(h, w_in, BF16, tm=1024, tn=1024, n_cols=main_cols, name="matmul_in")
        gates = _matmul(h, _pad_cols(w_in[:, main_cols:], LANES), F32, tm=1024, tn=LANES,
                        name="matmul_gates")
        proj = proj.reshape(B, S, main_cols)
        gates = gates.reshape(B, S, LANES)[:, :, :4 * dn_heads].reshape(B, S, 4, dn_heads)
        gates_col = jnp.repeat(gates.transpose(0, 3, 1, 2), GATE_REP, axis=-1)
        gates_row = gates.transpose(0, 3, 2, 1).reshape(B, dn_heads, 4, n_chunks, DN_CHUNK)
        o_diff = _diff_attention(proj, rope_c, rope_sa, rope_sb, diff_lambda[l], diff_subln_w[l],
                                 diff_heads, lambda_init)
        o_dn = _gated_deltanet(proj, 3 * diff_heads, 3 * diff_heads + 3 * dn_heads, dn_conv_w[l],
                               gates_col, gates_row, dn_a_log[l], dn_dt_bias[l], dn_norm_w[l], dn_heads)
        w_out = mix_w_out[l]
        mixed = _matmul_dual(o_diff.reshape(T, diff_width), o_dn.reshape(T, dn_width),
                             w_out[:diff_width].astype(BF16), w_out[diff_width:].astype(BF16),
                             tm=1024, tn=1024)
        xt, h = _post_pre(mixed, xt, mix_norms[l, 1], mem_norms[l, 0], 1.0)
        mlen = mem.shape[1]
        memn = _norm_cast(mem.reshape(B * mlen, D), mem_norms[l, 1])
        kv = _matmul(memn, mem_w_kv[l].astype(BF16), BF16, tm=1024, tn=1024, name="matmul_mem_kv")
        q = _matmul(h, mem_w_q[l].astype(BF16), BF16, tm=1024, tn=512, name="matmul_mem_q")
        width = MEM_HEADS * MEM_HDIM
        o_mem = _cross_attention(q.reshape(B, S, width), kv.reshape(B, mlen, 2 * width))
        c = _matmul(o_mem.reshape(T, width), mem_w_o[l].astype(BF16), BF16, tm=1024, tn=1024,
                    name="matmul_mem_o")
        xt, h = _post_pre(c, xt, mem_norms[l, 2], ffn2_norms[l, 0], 1.0)
        y = _ffn(h, *_ffn_weights(ffn2_w_gu[l], ffn2_w_down[l]))
        if l + 1 < depth:
            xt, h = _post_pre(y, xt, ffn2_norms[l, 1], ffn1_norms[l + 1, 0], 0.5)
        else:
            xt = _post_pre(y, xt, ffn2_norms[l, 1], None, 0.5)
    return xt.reshape(B, S, D)
```

```python
import functools
import math

import jax
import jax.numpy as jnp
from jax import lax
from jax.experimental import pallas as pl
from jax.experimental.pallas import tpu as pltpu

F32 = jnp.float32
BF16 = jnp.bfloat16

DIFF_VDIM = 128
DIFF_QKDIM = 64
DN_KDIM = 128
DN_VDIM = 128
DN_CONV = 5
DN_CHUNK = 64
DN_BLOCK = 16
DN_UNROLL = 32
ROPE_THETA = 500000.0
ROPE_DIM = DIFF_QKDIM // 4
MEM_HEADS = 4
MEM_HDIM = 128
NORM_EPS = 1e-6
SUBLN_EPS = 1e-5
LANES = 128
GATE_REP = LANES // 4
FF_PREP_BLOCK = 256
FF_ALIGN = 1024
VMEM_LIMIT = 56 * 1024 * 1024


def _cparams(*sem):
    return pltpu.CompilerParams(dimension_semantics=sem, vmem_limit_bytes=VMEM_LIMIT)


def _dot(a, b):
    return jnp.dot(a, b, preferred_element_type=F32)


def _dot_nt(a, b):
    return lax.dot_general(a, b, (((1,), (1,)), ((), ())), preferred_element_type=F32)


def _dot_tn(a, b):
    return lax.dot_general(a, b, (((0,), (0,)), ((), ())), preferred_element_type=F32)


def _dot_exact(a, b):
    return jnp.dot(a, b, preferred_element_type=F32, precision=lax.Precision.HIGHEST)


def _rms(x, w, eps):
    return x * lax.rsqrt(jnp.mean(x * x, axis=-1, keepdims=True) + eps) * w


def _silu(x):
    return x * jax.nn.sigmoid(x)


def _norm_kernel(x_ref, w_ref, o_ref):
    o_ref[...] = _rms(x_ref[...], w_ref[...], NORM_EPS).astype(o_ref.dtype)


def _norm_cast(x, w, tm=256):
    T, D = x.shape
    return pl.pallas_call(
        _norm_kernel,
        grid=(T // tm,),
        in_specs=[pl.BlockSpec((tm, D), lambda i: (i, 0)),
                  pl.BlockSpec((1, D), lambda i: (0, 0))],
        out_specs=pl.BlockSpec((tm, D), lambda i: (i, 0)),
        out_shape=jax.ShapeDtypeStruct((T, D), BF16),
        compiler_params=_cparams("parallel"),
        name="norm_cast",
    )(x, w.reshape(1, D))


def _post_pre_kernel(y_ref, x_ref, wpost_ref, wpre_ref, xo_ref, h_ref, *, scale):
    xn = x_ref[...] + scale * _rms(y_ref[...].astype(F32), wpost_ref[...], NORM_EPS)
    xo_ref[...] = xn
    h_ref[...] = _rms(xn, wpre_ref[...], NORM_EPS).astype(h_ref.dtype)


def _post_kernel(y_ref, x_ref, wpost_ref, xo_ref, *, scale):
    xo_ref[...] = x_ref[...] + scale * _rms(y_ref[...].astype(F32), wpost_ref[...], NORM_EPS)


def _post_pre(y, x, w_post, w_pre, scale, tm=256):
    T, D = x.shape
    row = pl.BlockSpec((tm, D), lambda i: (i, 0))
    vec = pl.BlockSpec((1, D), lambda i: (0, 0))
    if w_pre is None:
        return pl.pallas_call(
            functools.partial(_post_kernel, scale=scale),
            grid=(T // tm,),
            in_specs=[row, row, vec],
            out_specs=row,
            out_shape=jax.ShapeDtypeStruct((T, D), F32),
            compiler_params=_cparams("parallel"),
            name="post_norm_residual",
        )(y, x, w_post.reshape(1, D))
    return pl.pallas_call(
        functools.partial(_post_pre_kernel, scale=scale),
        grid=(T // tm,),
        in_specs=[row, row, vec, vec],
        out_specs=[row, row],
        out_shape=[jax.ShapeDtypeStruct((T, D), F32), jax.ShapeDtypeStruct((T, D), BF16)],
        compiler_params=_cparams("parallel"),
        name="post_norm_residual_pre_norm",
    )(y, x, w_post.reshape(1, D), w_pre.reshape(1, D))


def _mm_kernel(a_ref, w_ref, o_ref, *acc, nk):
    if nk == 1:
        o_ref[...] = _dot(a_ref[...], w_ref[...]).astype(o_ref.dtype)
        return
    acc_ref, = acc
    k = pl.program_id(2)

    @pl.when(k == 0)
    def _():
        acc_ref[...] = jnp.zeros_like(acc_ref)

    acc_ref[...] += _dot(a_ref[...], w_ref[...])

    @pl.when(k == nk - 1)
    def _():
        o_ref[...] = acc_ref[...].astype(o_ref.dtype)


def _matmul(a, w, out_dtype, tm, tn, tk=None, n_cols=None, name="matmul"):
    M, K = a.shape
    N = w.shape[1] if n_cols is None else n_cols
    tk = K if tk is None else tk
    tm, tn = min(tm, M), min(tn, N)
    nk = K // tk
    assert M % tm == 0 and N % tn == 0 and K % tk == 0
    return pl.pallas_call(
        functools.partial(_mm_kernel, nk=nk),
        grid=(M // tm, N // tn, nk),
        in_specs=[pl.BlockSpec((tm, tk), lambda i, j, k: (i, k)),
                  pl.BlockSpec((tk, tn), lambda i, j, k: (k, j))],
        out_specs=pl.BlockSpec((tm, tn), lambda i, j, k: (i, j)),
        out_shape=jax.ShapeDtypeStruct((M, N), out_dtype),
        scratch_shapes=[pltpu.VMEM((tm, tn), F32)] if nk > 1 else [],
        compiler_params=_cparams("parallel", "parallel", "arbitrary"),
        name=name,
    )(a, w)


def _mm_swiglu_kernel(a_ref, wg_ref, wu_ref, o_ref):
    a = a_ref[...]
    g = _dot(a, wg_ref[...])
    u = _dot(a, wu_ref[...])
    o_ref[...] = (_silu(g) * u).astype(o_ref.dtype)


def _matmul_swiglu(a, w_gu, tm, tn):
    M, K = a.shape
    N = w_gu.shape[1] // 2
    tm = min(tm, M)
    assert M % tm == 0 and N % tn == 0
    up0 = N // tn
    return pl.pallas_call(
        _mm_swiglu_kernel,
        grid=(M // tm, N // tn),
        in_specs=[pl.BlockSpec((tm, K), lambda i, j: (i, 0)),
                  pl.BlockSpec((K, tn), lambda i, j: (0, j)),
                  pl.BlockSpec((K, tn), lambda i, j: (0, up0 + j))],
        out_specs=pl.BlockSpec((tm, tn), lambda i, j: (i, j)),
        out_shape=jax.ShapeDtypeStruct((M, N), BF16),
        compiler_params=_cparams("parallel", "parallel"),
        name="matmul_swiglu",
    )(a, w_gu, w_gu)


def _cast_pad_kernel(x_ref, o_ref, *, axis, n_real, n_padded):
    j = pl.program_id(axis) % n_padded

    @pl.when(j < n_real)
    def _():
        o_ref[...] = x_ref[...].astype(o_ref.dtype)

    @pl.when(j >= n_real)
    def _():
        o_ref[...] = jnp.zeros_like(o_ref)


def _cast_pad_halves(w, halves, n_real, n_padded, bw=256):
    K = w.shape[0]
    src = lambda c: (c // n_padded) * n_real + jnp.minimum(c % n_padded, n_real - 1)
    return pl.pallas_call(
        functools.partial(_cast_pad_kernel, axis=0, n_real=n_real, n_padded=n_padded),
        grid=(halves * n_padded,),
        in_specs=[pl.BlockSpec((K, bw), lambda c: (0, src(c)))],
        out_specs=pl.BlockSpec((K, bw), lambda c: (0, c)),
        out_shape=jax.ShapeDtypeStruct((K, halves * n_padded * bw), BF16),
        compiler_params=_cparams("parallel"),
        name="cast_pad_cols",
    )(w)


def _cast_pad_rows(w, n_real, n_padded, bh=256):
    N = w.shape[1]
    return pl.pallas_call(
        functools.partial(_cast_pad_kernel, axis=0, n_real=n_real, n_padded=n_padded),
        grid=(n_padded,),
        in_specs=[pl.BlockSpec((bh, N), lambda r: (jnp.minimum(r, n_real - 1), 0))],
        out_specs=pl.BlockSpec((bh, N), lambda r: (r, 0)),
        out_shape=jax.ShapeDtypeStruct((n_padded * bh, N), BF16),
        compiler_params=_cparams("parallel"),
        name="cast_pad_rows",
    )(w)


def _mm_dual_kernel(a1_ref, a2_ref, w1_ref, w2_ref, o_ref):
    o_ref[...] = (_dot(a1_ref[...], w1_ref[...]) + _dot(a2_ref[...], w2_ref[...])).astype(o_ref.dtype)


def _matmul_dual(a1, a2, w1, w2, tm, tn):
    M, K1 = a1.shape
    K2 = a2.shape[1]
    N = w1.shape[1]
    tm = min(tm, M)
    assert M % tm == 0 and N % tn == 0
    return pl.pallas_call(
        _mm_dual_kernel,
        grid=(M // tm, N // tn),
        in_specs=[pl.BlockSpec((tm, K1), lambda i, j: (i, 0)),
                  pl.BlockSpec((tm, K2), lambda i, j: (i, 0)),
                  pl.BlockSpec((K1, tn), lambda i, j: (0, j)),
                  pl.BlockSpec((K2, tn), lambda i, j: (0, j))],
        out_specs=pl.BlockSpec((tm, tn), lambda i, j: (i, j)),
        out_shape=jax.ShapeDtypeStruct((M, N), BF16),
        compiler_params=_cparams("parallel", "parallel"),
        name="matmul_dual",
    )(a1, a2, w1, w2)


def _ffn(h, w_gu, w_down):
    act = _matmul_swiglu(h, w_gu, tm=1024, tn=512)
    kf = w_down.shape[0]
    return _matmul(act, w_down, BF16, tm=1024, tn=1024, tk=kf // 4, name="matmul_down")


def _diff_attn_kernel(lam_ref, q_ref, k_ref, v_ref, c_ref, sa_ref, sb_ref, w_ref, o_ref,
                      qs_ref, ks_ref, vs_ref, *, tq, lambda_init):
    S = q_ref.shape[0]
    c, sa, sb = c_ref[...], sa_ref[...], sb_ref[...]

    def rope(x):
        half = ROPE_DIM // 2
        return x * c + pltpu.roll(x, half, 1) * sa + pltpu.roll(x, LANES - half, 1) * sb

    q = rope(q_ref[...].astype(F32)) * (DIFF_QKDIM ** -0.5)
    k = rope(k_ref[...].astype(F32))
    first_map = lax.broadcasted_iota(jnp.int32, (S, LANES), 1) < DIFF_QKDIM
    qs_ref[0] = jnp.where(first_map, q, 0.0).astype(BF16)
    qs_ref[1] = jnp.where(first_map, 0.0, q).astype(BF16)
    ks_ref[...] = k.astype(BF16)
    vs_ref[...] = v_ref[...].astype(BF16)

    lp = lam_ref[...]
    lam = (jnp.exp(jnp.sum(lp[0:1] * lp[1:2], keepdims=True))
           - jnp.exp(jnp.sum(lp[2:3] * lp[3:4], keepdims=True)) + lambda_init)
    w = w_ref[...]

    def softmax_pv(s):
        e = jnp.exp(s - jnp.max(s, axis=-1, keepdims=True))
        l = jnp.sum(e, axis=-1, keepdims=True)
        return _dot(e.astype(BF16), vs_ref[...]) / l

    def scores(m, i):
        return _dot_nt(qs_ref[m, i * tq:(i + 1) * tq, :], ks_ref[...])

    nq = S // tq
    s1, s2 = scores(0, 0), scores(1, 0)
    for i in range(nq):
        last = i + 1 == nq
        s1_next = None if last else scores(0, i + 1)
        o1 = softmax_pv(s1)
        s2_next = None if last else scores(1, i + 1)
        o = o1 - lam * softmax_pv(s2)
        o = _rms(o, w, SUBLN_EPS) * (1.0 - lambda_init)
        o_ref[i * tq:(i + 1) * tq, :] = o.astype(o_ref.dtype)
        s1, s2 = s1_next, s2_next


def _diff_attention(proj, rope_c, rope_sa, rope_sb, lam_params, subln_w, heads, lambda_init, tq=512):
    B, S, _ = proj.shape
    tq = min(tq, S)
    head = lambda off: pl.BlockSpec((None, S, LANES), lambda b, h: (b, 0, off + h))
    table = pl.BlockSpec((None, S, LANES), lambda b, h: (b, 0, 0))
    return pl.pallas_call(
        functools.partial(_diff_attn_kernel, tq=tq, lambda_init=lambda_init),
        grid=(B, heads),
        in_specs=[pl.BlockSpec((4, DIFF_QKDIM), lambda b, h: (0, 0)),
                  head(0), head(heads), head(2 * heads), table, table, table,
                  pl.BlockSpec((1, DIFF_VDIM), lambda b, h: (0, 0))],
        out_specs=pl.BlockSpec((None, S, LANES), lambda b, h: (b, 0, h)),
        out_shape=jax.ShapeDtypeStruct((B, S, heads * DIFF_VDIM), BF16),
        scratch_shapes=[pltpu.VMEM((2, S, LANES), BF16), pltpu.VMEM((S, LANES), BF16),
                        pltpu.VMEM((S, LANES), BF16)],
        compiler_params=_cparams("parallel", "parallel"),
        name="diff_attention",
    )(lam_params, proj, proj, proj, rope_c, rope_sa, rope_sb, subln_w.reshape(1, DIFF_VDIM))


def _bdot(a, b):
    return lax.dot_general(a, b, (((2,), (1,)), ((0,), (0,))), preferred_element_type=F32)


def _bdot_nt(a, b):
    return lax.dot_general(a, b, (((2,), (2,)), ((0,), (0,))), preferred_element_type=F32)


def _bdot_tn(a, b):
    return lax.dot_general(a, b, (((1,), (1,)), ((0,), (0,))), preferred_element_type=F32)


def _unit_tri_inverse(a, eye, same_block):
    mm = lambda x, y: _bdot(x.astype(BF16), y.astype(BF16))
    ad = jnp.where(same_block, a, 0.0)
    ao = a - ad
    p = eye - ad
    sq = ad
    for _ in range(int(math.log2(DN_BLOCK)) - 1):
        sq = mm(sq, sq)
        p = p + mm(p, sq)
    b = mm(p, ao)
    b2 = mm(b, b)
    x = eye - b + b2 - mm(b, b2)
    return mm(x, p)


def _gdn_kernel(alog_ref, dtb_ref, q_ref, k_ref, v_ref, z_ref, cq_ref, ck_ref, cv_ref,
                gcol_ref, grow_ref, nw_ref, o_ref,
                q16_ref, k16_ref, gc_ref, beta_ref, rhs_ref, kt_ref, egl_ref, gcr_ref,
                u_ref, lhs_ref, qkd_ref, qm_ref, vn_ref, od_ref):
    S = q_ref.shape[0]
    C = DN_CHUNK
    N = S // C
    h = pl.program_id(1)

    t = lax.broadcasted_iota(jnp.int32, (S, LANES), 0)
    pad = (DN_CONV - 1) // 2

    def conv_silu(x_ref, cw_ref):
        x = x_ref[...].astype(F32)
        cw = cw_ref[...].astype(F32)
        y = x * cw[pad:pad + 1]
        for i in range(DN_CONV):
            d = i - pad
            if d == 0:
                continue
            xs = pltpu.roll(x, (-d) % S, 0)
            valid = (t + d >= 0) & (t + d < S)
            y = y + jnp.where(valid, xs, 0.0) * cw[i:i + 1]
        return _silu(y)

    def l2n(x):
        return x * lax.rsqrt(jnp.sum(x * x, axis=-1, keepdims=True) + 1e-6)

    qn = l2n(conv_silu(q_ref, cq_ref)) * (DN_KDIM ** -0.5)
    kn = l2n(conv_silu(k_ref, ck_ref))
    vn = conv_silu(v_ref, cv_ref)
    q16_ref[...] = qn.astype(BF16)
    k16_ref[...] = kn.astype(BF16)

    ri = lax.broadcasted_iota(jnp.int32, (C, C), 0)
    ci = lax.broadcasted_iota(jnp.int32, (C, C), 1)
    eye = (ri == ci).astype(F32)
    same_block = (ri // DN_BLOCK) == (ci // DN_BLOCK)
    incl = (ri >= ci, ri <= ci)
    strict = (ri > ci, ri < ci)
    tri_row = (incl[1].astype(F32), incl[0].astype(F32))

    def softplus(x):
        return jnp.maximum(x, 0.0) + jnp.log1p(jnp.exp(-jnp.abs(x)))

    def log_decay(x, d):
        a = jnp.exp(jnp.full((1, 1), alog_ref[d, h], F32))
        return -a * softplus(x + dtb_ref[d, h])

    pos = t % C
    grp = lax.broadcasted_iota(jnp.int32, (1, LANES), 1) // GATE_REP
    per_dir = lambda ref: jnp.where(grp == 0, ref[0, h], jnp.where(grp == 2, ref[1, h], 0.0))
    chunked = lambda m: m.reshape(N, C, LANES)
    gx = gcol_ref[...].astype(F32)
    g = -jnp.exp(per_dir(alog_ref)) * softplus(gx + per_dir(dtb_ref))
    beta_x = jax.nn.sigmoid(gx)
    pre = g
    shift = 1
    while shift < C:
        pre = pre + jnp.where(pos >= shift, pltpu.roll(pre, shift, 0), 0.0)
        shift *= 2
    pre3 = chunked(pre)
    tot = pre3[:, C - 1:C, :]
    gc_x = jnp.where(grp == 2, tot - pre3 + chunked(g), pre3).reshape(S, LANES)
    from_group = lambda m, q: jnp.broadcast_to(m[:, q * GATE_REP:q * GATE_REP + 1], (S, LANES))
    for d in range(2):
        beta = from_group(beta_x, 2 * d + 1)
        gc = from_group(gc_x, 2 * d)
        gc3 = chunked(gc)
        g_last = gc3[:, C - 1:C, :] if d == 0 else gc3[:, 0:1, :]
        egc = jnp.exp(gc)
        gc_ref[d] = gc
        beta_ref[d] = beta
        rhs_ref[d, :, :LANES] = (vn * beta).astype(BF16)
        rhs_ref[d, :, LANES:] = (kn * beta * egc).astype(BF16)
        kt_ref[d] = (chunked(kn) * jnp.exp(g_last - gc3)).reshape(S, LANES).astype(BF16)
        lhs_ref[d, :, DN_KDIM + C:, :] = chunked(qn * egc).astype(BF16)
        egl_ref[d] = jnp.broadcast_to(jnp.exp(g_last), (N, 8, LANES))
        g_row = log_decay(grow_ref[2 * d].astype(F32), d)
        gc_row = _dot_exact(g_row, tri_row[d])
        for n in range(N):
            gcr_ref[d, n] = jnp.broadcast_to(gc_row[n:n + 1, :], (8, C))

    U = min(DN_UNROLL, N)

    def factor_body(i, carry):
        n0 = pl.multiple_of(i * U, U)
        rows = pl.ds(pl.multiple_of(i * (U * C), U * C), U * C)
        chunks = pl.ds(n0, U)
        per_chunk = lambda m: m.reshape(U, C, m.shape[-1])
        k16 = per_chunk(k16_ref[rows, :])
        kq = _bdot_nt(jnp.concatenate([k16, per_chunk(q16_ref[rows, :])], axis=1), k16)
        kk, qk = kq[:, :C], kq[:, C:]
        a = []
        for d in range(2):
            gcc = per_chunk(gc_ref[d, rows, :])[:, :, :C]
            diff = jnp.where(incl[d], gcc - gcr_ref[d, chunks][:, 0:1, :], 0.0)
            decay = jnp.where(incl[d], jnp.exp(diff), 0.0)
            qkd_ref[d, chunks] = (qk * decay).astype(BF16)
            a.append(jnp.where(strict[d], per_chunk(beta_ref[d, rows, :])[:, :, :C] * kk * decay, 0.0))
        tinv = _unit_tri_inverse(jnp.concatenate(a, axis=0), eye, same_block)
        rhs = jnp.concatenate([per_chunk(rhs_ref[d, rows, :]) for d in range(2)], axis=0)
        uw = _bdot(tinv.astype(BF16), rhs)
        uw16 = uw.astype(BF16)
        kt = jnp.concatenate([per_chunk(kt_ref[d, rows, :]) for d in range(2)], axis=0)
        pq = _bdot_tn(kt, uw16)
        for d in range(2):
            sel = slice(d * U, (d + 1) * U)
            u_ref[d, rows, :] = uw[sel, :, :LANES].reshape(U * C, LANES)
            qm_ref[d, chunks] = pq[sel, :, :LANES]
            lhs_ref[d, chunks, :DN_KDIM, :] = pq[sel, :, LANES:].astype(BF16)
            lhs_ref[d, chunks, DN_KDIM:DN_KDIM + C, :] = uw16[sel, :, LANES:]
        return carry

    lax.fori_loop(0, N // U, factor_body, 0)

    def scan_step(d, n, state):
        rows = pl.ds(pl.multiple_of(n * C, C), C)
        prod = _dot(lhs_ref[d, n], state.astype(BF16))
        vn_ref[d, rows, :] = (u_ref[d, rows, :] - prod[DN_KDIM:DN_KDIM + C]).astype(BF16)
        od_ref[d, rows, :] = prod[DN_KDIM + C:]
        return state * egl_ref[d, n][0:1, :] - prod[:DN_KDIM] + qm_ref[d, n]

    def scan_body(n, states):
        return scan_step(0, n, states[0]), scan_step(1, N - 1 - n, states[1])

    zero = jnp.zeros((DN_KDIM, DN_VDIM), F32)
    lax.fori_loop(0, N, scan_body, (zero, zero), unroll=2)

    def intra_body(i, carry):
        rows = pl.ds(pl.multiple_of(i * (U * C), U * C), U * C)
        chunks = pl.ds(pl.multiple_of(i * U, U), U)
        for d in range(2):
            intra = _bdot(qkd_ref[d, chunks], vn_ref[d, rows, :].reshape(U, C, LANES))
            od_ref[d, rows, :] += intra.reshape(U * C, LANES)
        return carry

    lax.fori_loop(0, N // U, intra_body, 0)

    o = od_ref[0] + od_ref[1]
    o = _rms(o, nw_ref[...], NORM_EPS) * _silu(z_ref[...].astype(F32))
    o_ref[...] = o.astype(o_ref.dtype)


def _gated_deltanet(proj, qkv_block0, z_block0, conv_w, gates_col, gates_row, a_log, dt_bias, norm_w, heads):
    B, S, _ = proj.shape
    N = S // DN_CHUNK
    head = lambda off: pl.BlockSpec((None, S, LANES), lambda b, h: (b, 0, off + h))
    cw = lambda off: pl.BlockSpec((DN_CONV, LANES), lambda b, h: (0, off + h))
    smem = pl.BlockSpec(memory_space=pltpu.SMEM)
    return pl.pallas_call(
        _gdn_kernel,
        grid=(B, heads),
        in_specs=[smem, smem,
                  head(qkv_block0), head(qkv_block0 + heads), head(qkv_block0 + 2 * heads), head(z_block0),
                  cw(0), cw(heads), cw(2 * heads),
                  pl.BlockSpec((None, None, S, LANES), lambda b, h: (b, h, 0, 0)),
                  pl.BlockSpec((None, None, 4, N, DN_CHUNK), lambda b, h: (b, h, 0, 0, 0)),
                  pl.BlockSpec((1, DN_VDIM), lambda b, h: (0, 0))],
        out_specs=pl.BlockSpec((None, S, LANES), lambda b, h: (b, 0, h)),
        out_shape=jax.ShapeDtypeStruct((B, S, heads * DN_VDIM), BF16),
        scratch_shapes=[
            pltpu.VMEM((S, LANES), BF16),
            pltpu.VMEM((S, LANES), BF16),
            pltpu.VMEM((2, S, LANES), F32),
            pltpu.VMEM((2, S, LANES), F32),
            pltpu.VMEM((2, S, 2 * LANES), BF16),
            pltpu.VMEM((2, S, LANES), BF16),
            pltpu.VMEM((2, N, 8, LANES), F32),
            pltpu.VMEM((2, N, 8, DN_CHUNK), F32),
            pltpu.VMEM((2, S, LANES), F32),
            pltpu.VMEM((2, N, DN_KDIM + 2 * DN_CHUNK, LANES), BF16),
            pltpu.VMEM((2, N, DN_CHUNK, DN_CHUNK), BF16),
            pltpu.VMEM((2, N, DN_KDIM, LANES), F32),
            pltpu.VMEM((2, S, LANES), BF16),
            pltpu.VMEM((2, S, LANES), F32),
        ],
        compiler_params=_cparams("parallel", "parallel"),
        name="gated_deltanet",
    )(a_log, dt_bias, proj, proj, proj, proj, conv_w, conv_w, conv_w, gates_col, gates_row,
      norm_w.reshape(1, DN_VDIM))


def _cross_attn_kernel(q_ref, kv_ref, o_ref):
    width = MEM_HEADS * MEM_HDIM
    outs = []
    for hd in range(MEM_HEADS):
        cols = slice(hd * MEM_HDIM, (hd + 1) * MEM_HDIM)
        q = q_ref[:, cols]
        k = kv_ref[:, cols]
        v = kv_ref[:, width + hd * MEM_HDIM:width + (hd + 1) * MEM_HDIM]
        s = _dot_nt(q, k) * (MEM_HDIM ** -0.5)
        e = jnp.exp(s - jnp.max(s, axis=-1, keepdims=True))
        l = jnp.sum(e, axis=-1, keepdims=True)
        outs.append(_dot(e.astype(BF16), v) / l)
    o_ref[...] = jnp.concatenate(outs, axis=1).astype(o_ref.dtype)


def _cross_attention(q, kv, tq=512):
    B, S, W = q.shape
    M = kv.shape[1]
    tq = min(tq, S)
    return pl.pallas_call(
        _cross_attn_kernel,
        grid=(B, S // tq),
        in_specs=[pl.BlockSpec((None, tq, W), lambda b, i: (b, i, 0)),
                  pl.BlockSpec((None, M, 2 * W), lambda b, i: (b, 0, 0))],
        out_specs=pl.BlockSpec((None, tq, W), lambda b, i: (b, i, 0)),
        out_shape=jax.ShapeDtypeStruct((B, S, W), BF16),
        compiler_params=_cparams("parallel", "parallel"),
        name="cross_attention",
    )(q, kv)


def _rope_tables(positions):
    B, S = positions.shape
    half = ROPE_DIM // 2
    inv_freq = ROPE_THETA ** (-jnp.arange(0, ROPE_DIM, 2, dtype=F32) / ROPE_DIM)
    ang = positions.astype(F32)[..., None] * inv_freq
    cos, sin = jnp.cos(ang), jnp.sin(ang)
    rest = DIFF_QKDIM - ROPE_DIM
    ones, zeros = jnp.ones((B, S, rest), F32), jnp.zeros((B, S, rest), F32)
    z8 = jnp.zeros((B, S, half), F32)
    both = lambda m: jnp.concatenate([m, m], axis=-1)
    c = both(jnp.concatenate([cos, cos, ones], axis=-1))
    sa = both(jnp.concatenate([z8, sin, zeros], axis=-1))
    sb = both(jnp.concatenate([-sin, z8, zeros], axis=-1))
    return c, sa, sb


def _pad_cols(w, n):
    return jnp.pad(w, ((0, 0), (0, n - w.shape[1])))


def _ffn_weights(w_gu, w_down):
    d_ff = w_down.shape[0]
    d_pad = -(-d_ff // FF_ALIGN) * FF_ALIGN
    assert d_ff % FF_PREP_BLOCK == 0
    n_real, n_padded = d_ff // FF_PREP_BLOCK, d_pad // FF_PREP_BLOCK
    return (_cast_pad_halves(w_gu, 2, n_real, n_padded, FF_PREP_BLOCK),
            _cast_pad_rows(w_down, n_real, n_padded, FF_PREP_BLOCK))


def kernel(x, mem, positions, ffn1_norms, ffn1_w_gu, ffn1_w_down, mix_norms, mix_w_in, dn_conv_w, dn_a_log, dn_dt_bias, dn_norm_w, diff_lambda, diff_subln_w, mix_w_out, mem_norms, mem_w_q, mem_w_kv, mem_w_o, ffn2_norms, ffn2_w_gu, ffn2_w_down):
    B, S, D = x.shape
    T = B * S
    depth = ffn1_norms.shape[0]
    diff_width = D // 2
    dn_width = D - diff_width
    diff_heads = diff_width // DIFF_VDIM
    dn_heads = dn_width // DN_VDIM
    main_cols = 3 * diff_width + 4 * dn_width
    n_chunks = S // DN_CHUNK

    rope_c, rope_sa, rope_sb = _rope_tables(positions)
    xt = x.reshape(T, D)
    h = _norm_cast(xt, ffn1_norms[0, 0])
    for l in range(depth):
        lambda_init = 0.8 - 0.6 * math.exp(-0.3 * l)
        y = _ffn(h, *_ffn_weights(ffn1_w_gu[l], ffn1_w_down[l]))
        xt, h = _post_pre(y, xt, ffn1_norms[l, 1], mix_norms[l, 0], 0.5)
        w_in = mix_w_in[l].astype(BF16)
        proj = _matmul(h, w_in, BF16, tm=1024, tn=1024, n_cols=main_cols, name="matmul_in")
        gates = _matmul(h, _pad_cols(w_in[:, main_cols:], LANES), F32, tm=1024, tn=LANES,
                        name="matmul_gates")
        proj = proj.reshape(B, S, main_cols)
        gates = gates.reshape(B, S, LANES)[:, :, :4 * dn_heads].reshape(B, S, 4, dn_heads)
        gates_col = jnp.repeat(gates.transpose(0, 3, 1, 2), GATE_REP, axis=-1)
        gates_row = gates.transpose(0, 3, 2, 1).reshape(B, dn_heads, 4, n_chunks, DN_CHUNK)
        o_diff = _diff_attention(proj, rope_c, rope_sa, rope_sb, diff_lambda[l], diff_subln_w[l],
                                 diff_heads, lambda_init)
        o_dn = _gated_deltanet(proj, 3 * diff_heads, 3 * diff_heads + 3 * dn_heads, dn_conv_w[l],
                               gates_col, gates_row, dn_a_log[l], dn_dt_bias[l], dn_norm_w[l], dn_heads)
        w_out = mix_w_out[l]
        mixed = _matmul_dual(o_diff.reshape(T, diff_width), o_dn.reshape(T, dn_width),
                             w_out[:diff_width].astype(BF16), w_out[diff_width:].astype(BF16),
                             tm=1024, tn=1024)
        xt, h = _post_pre(mixed, xt, mix_norms[l, 1], mem_norms[l, 0], 1.0)
        mlen = mem.shape[1]
        memn = _norm_cast(mem.reshape(B * mlen, D), mem_norms[l, 1])
        kv = _matmul(memn, mem_w_kv[l].astype(BF16), BF16, tm=1024, tn=1024, name="matmul_mem_kv")
        q = _matmul(h, mem_w_q[l].astype(BF16), BF16, tm=1024, tn=512, name="matmul_mem_q")
        width = MEM_HEADS * MEM_HDIM
        o_mem = _cross_attention(q.reshape(B, S, width), kv.reshape(B, mlen, 2 * width))
        c = _matmul(o_mem.reshape(T, width), mem_w_o[l].astype(BF16), BF16, tm=1024, tn=1024,
                    name="matmul_mem_o")
        xt, h = _post_pre(c, xt, mem_norms[l, 2], ffn2_norms[l, 0], 1.0)
        y = _ffn(h, *_ffn_weights(ffn2_w_gu[l], ffn2_w_down[l]))
        if l + 1 < depth:
            xt, h = _post_pre(y, xt, ffn2_norms[l, 1], ffn1_norms[l + 1, 0], 0.5)
        else:
            xt = _post_pre(y, xt, ffn2_norms[l, 1], None, 0.5)
    return xt.reshape(B, S, D)
```

```python
import functools
import math

import jax
import jax.numpy as jnp
from jax import lax
from jax.experimental import pallas as pl
from jax.experimental.pallas import tpu as pltpu

F32 = jnp.float32
BF16 = jnp.bfloat16

DIFF_VDIM = 128
DIFF_QKDIM = 64
DN_KDIM = 128
DN_VDIM = 128
DN_CONV = 5
DN_CHUNK = 64
DN_BLOCK = 16
DN_UNROLL = 32
ROPE_THETA = 500000.0
ROPE_DIM = DIFF_QKDIM // 4
MEM_HEADS = 4
MEM_HDIM = 128
NORM_EPS = 1e-6
SUBLN_EPS = 1e-5
LANES = 128
GATE_REP = LANES // 4
FF_PREP_BLOCK = 256
FF_ALIGN = 1024
VMEM_LIMIT = 56 * 1024 * 1024


def _cparams(*sem):
    return pltpu.CompilerParams(dimension_semantics=sem, vmem_limit_bytes=VMEM_LIMIT)


def _dot(a, b):
    return jnp.dot(a, b, preferred_element_type=F32)


def _dot_nt(a, b):
    return lax.dot_general(a, b, (((1,), (1,)), ((), ())), preferred_element_type=F32)


def _dot_tn(a, b):
    return lax.dot_general(a, b, (((0,), (0,)), ((), ())), preferred_element_type=F32)


def _dot_exact(a, b):
    return jnp.dot(a, b, preferred_element_type=F32, precision=lax.Precision.HIGHEST)


def _rms(x, w, eps):
    return x * lax.rsqrt(jnp.mean(x * x, axis=-1, keepdims=True) + eps) * w


def _silu(x):
    return x * jax.nn.sigmoid(x)


def _norm_kernel(x_ref, w_ref, o_ref):
    o_ref[...] = _rms(x_ref[...], w_ref[...], NORM_EPS).astype(o_ref.dtype)


def _norm_cast(x, w, tm=256):
    T, D = x.shape
    return pl.pallas_call(
        _norm_kernel,
        grid=(T // tm,),
        in_specs=[pl.BlockSpec((tm, D), lambda i: (i, 0)),
                  pl.BlockSpec((1, D), lambda i: (0, 0))],
        out_specs=pl.BlockSpec((tm, D), lambda i: (i, 0)),
        out_shape=jax.ShapeDtypeStruct((T, D), BF16),
        compiler_params=_cparams("parallel"),
        name="norm_cast",
    )(x, w.reshape(1, D))


def _post_pre_kernel(y_ref, x_ref, wpost_ref, wpre_ref, xo_ref, h_ref, *, scale):
    xn = x_ref[...] + scale * _rms(y_ref[...].astype(F32), wpost_ref[...], NORM_EPS)
    xo_ref[...] = xn
    h_ref[...] = _rms(xn, wpre_ref[...], NORM_EPS).astype(h_ref.dtype)


def _post_kernel(y_ref, x_ref, wpost_ref, xo_ref, *, scale):
    xo_ref[...] = x_ref[...] + scale * _rms(y_ref[...].astype(F32), wpost_ref[...], NORM_EPS)


def _post_pre(y, x, w_post, w_pre, scale, tm=256):
    T, D = x.shape
    row = pl.BlockSpec((tm, D), lambda i: (i, 0))
    vec = pl.BlockSpec((1, D), lambda i: (0, 0))
    if w_pre is None:
        return pl.pallas_call(
            functools.partial(_post_kernel, scale=scale),
            grid=(T // tm,),
            in_specs=[row, row, vec],
            out_specs=row,
            out_shape=jax.ShapeDtypeStruct((T, D), F32),
            compiler_params=_cparams("parallel"),
            name="post_norm_residual",
        )(y, x, w_post.reshape(1, D))
    return pl.pallas_call(
        functools.partial(_post_pre_kernel, scale=scale),
        grid=(T // tm,),
        in_specs=[row, row, vec, vec],
        out_specs=[row, row],
        out_shape=[jax.ShapeDtypeStruct((T, D), F32), jax.ShapeDtypeStruct((T, D), BF16)],
        compiler_params=_cparams("parallel"),
        name="post_norm_residual_pre_norm",
    )(y, x, w_post.reshape(1, D), w_pre.reshape(1, D))


def _mm_kernel(a_ref, w_ref, o_ref, *acc, nk):
    if nk == 1:
        o_ref[...] = _dot(a_ref[...], w_ref[...]).astype(o_ref.dtype)
        return
    acc_ref, = acc
    k = pl.program_id(2)

    @pl.when(k == 0)
    def _():
        acc_ref[...] = jnp.zeros_like(acc_ref)

    acc_ref[...] += _dot(a_ref[...], w_ref[...])

    @pl.when(k == nk - 1)
    def _():
        o_ref[...] = acc_ref[...].astype(o_ref.dtype)


def _matmul(a, w, out_dtype, tm, tn, tk=None, n_cols=None, name="matmul"):
    M, K = a.shape
    N = w.shape[1] if n_cols is None else n_cols
    tk = K if tk is None else tk
    tm, tn = min(tm, M), min(tn, N)
    nk = K // tk
    assert M % tm == 0 and N % tn == 0 and K % tk == 0
    return pl.pallas_call(
        functools.partial(_mm_kernel, nk=nk),
        grid=(M // tm, N // tn, nk),
        in_specs=[pl.BlockSpec((tm, tk), lambda i, j, k: (i, k)),
                  pl.BlockSpec((tk, tn), lambda i, j, k: (k, j))],
        out_specs=pl.BlockSpec((tm, tn), lambda i, j, k: (i, j)),
        out_shape=jax.ShapeDtypeStruct((M, N), out_dtype),
        scratch_shapes=[pltpu.VMEM((tm, tn), F32)] if nk > 1 else [],
        compiler_params=_cparams("parallel", "parallel", "arbitrary"),
        name=name,
    )(a, w)


def _mm_swiglu_kernel(a_ref, wg_ref, wu_ref, o_ref):
    a = a_ref[...]
    g = _dot(a, wg_ref[...])
    u = _dot(a, wu_ref[...])
    o_ref[...] = (_silu(g) * u).astype(o_ref.dtype)


def _matmul_swiglu(a, w_gu, tm, tn):
    M, K = a.shape
    N = w_gu.shape[1] // 2
    tm = min(tm, M)
    assert M % tm == 0 and N % tn == 0
    up0 = N // tn
    return pl.pallas_call(
        _mm_swiglu_kernel,
        grid=(M // tm, N // tn),
        in_specs=[pl.BlockSpec((tm, K), lambda i, j: (i, 0)),
                  pl.BlockSpec((K, tn), lambda i, j: (0, j)),
                  pl.BlockSpec((K, tn), lambda i, j: (0, up0 + j))],
        out_specs=pl.BlockSpec((tm, tn), lambda i, j: (i, j)),
        out_shape=jax.ShapeDtypeStruct((M, N), BF16),
        compiler_params=_cparams("parallel", "parallel"),
        name="matmul_swiglu",
    )(a, w_gu, w_gu)


def _cast_pad_kernel(x_ref, o_ref, *, axis, n_real, n_padded):
    j = pl.program_id(axis) % n_padded

    @pl.when(j < n_real)
    def _():
        o_ref[...] = x_ref[...].astype(o_ref.dtype)

    @pl.when(j >= n_real)
    def _():
        o_ref[...] = jnp.zeros_like(o_ref)


def _cast_pad_halves(w, halves, n_real, n_padded, bw=256):
    K = w.shape[0]
    src = lambda c: (c // n_padded) * n_real + jnp.minimum(c % n_padded, n_real - 1)
    return pl.pallas_call(
        functools.partial(_cast_pad_kernel, axis=0, n_real=n_real, n_padded=n_padded),
        grid=(halves * n_padded,),
        in_specs=[pl.BlockSpec((K, bw), lambda c: (0, src(c)))],
        out_specs=pl.BlockSpec((K, bw), lambda c: (0, c)),
        out_shape=jax.ShapeDtypeStruct((K, halves * n_padded * bw), BF16),
        compiler_params=_cparams("parallel"),
        name="cast_pad_cols",
    )(w)


def _cast_pad_rows(w, n_real, n_padded, bh=256):
    N = w.shape[1]
    return pl.pallas_call(
        functools.partial(_cast_pad_kernel, axis=0, n_real=n_real, n_padded=n_padded),
        grid=(n_padded,),
        in_specs=[pl.BlockSpec((bh, N), lambda r: (jnp.minimum(r, n_real - 1), 0))],
        out_specs=pl.BlockSpec((bh, N), lambda r: (r, 0)),
        out_shape=jax.ShapeDtypeStruct((n_padded * bh, N), BF16),
        compiler_params=_cparams("parallel"),
        name="cast_pad_rows",
    )(w)


def _mm_dual_kernel(a1_ref, a2_ref, w1_ref, w2_ref, o_ref):
    o_ref[...] = (_dot(a1_ref[...], w1_ref[...]) + _dot(a2_ref[...], w2_ref[...])).astype(o_ref.dtype)


def _matmul_dual(a1, a2, w1, w2, tm, tn):
    M, K1 = a1.shape
    K2 = a2.shape[1]
    N = w1.shape[1]
    tm = min(tm, M)
    assert M % tm == 0 and N % tn == 0
    return pl.pallas_call(
        _mm_dual_kernel,
        grid=(M // tm, N // tn),
        in_specs=[pl.BlockSpec((tm, K1), lambda i, j: (i, 0)),
                  pl.BlockSpec((tm, K2), lambda i, j: (i, 0)),
                  pl.BlockSpec((K1, tn), lambda i, j: (0, j)),
                  pl.BlockSpec((K2, tn), lambda i, j: (0, j))],
        out_specs=pl.BlockSpec((tm, tn), lambda i, j: (i, j)),
        out_shape=jax.ShapeDtypeStruct((M, N), BF16),
        compiler_params=_cparams("parallel", "parallel"),
        name="matmul_dual",
    )(a1, a2, w1, w2)


def _ffn(h, w_gu, w_down):
    act = _matmul_swiglu(h, w_gu, tm=1024, tn=512)
    kf = w_down.shape[0]
    return _matmul(act, w_down, BF16, tm=1024, tn=1024, tk=kf // 4, name="matmul_down")


def _diff_attn_kernel(lam_ref, q_ref, k_ref, v_ref, c_ref, sa_ref, sb_ref, w_ref, o_ref,
                      qs_ref, ks_ref, vs_ref, *, tq, lambda_init):
    S = q_ref.shape[0]
    c, sa, sb = c_ref[...], sa_ref[...], sb_ref[...]

    def rope(x):
        half = ROPE_DIM // 2
        return x * c + pltpu.roll(x, half, 1) * sa + pltpu.roll(x, LANES - half, 1) * sb

    q = rope(q_ref[...].astype(F32)) * (DIFF_QKDIM ** -0.5)
    k = rope(k_ref[...].astype(F32))
    first_map = lax.broadcasted_iota(jnp.int32, (S, LANES), 1) < DIFF_QKDIM
    qs_ref[0] = jnp.where(first_map, q, 0.0).astype(BF16)
    qs_ref[1] = jnp.where(first_map, 0.0, q).astype(BF16)
    ks_ref[...] = k.astype(BF16)
    vs_ref[:, :LANES] = v_ref[...].astype(BF16)
    vs_ref[:, LANES:] = jnp.ones((S, LANES), BF16)

    lp = lam_ref[...]
    lam = (jnp.exp(jnp.sum(lp[0:1] * lp[1:2], keepdims=True))
           - jnp.exp(jnp.sum(lp[2:3] * lp[3:4], keepdims=True)) + lambda_init)
    w = w_ref[...]

    def softmax_pv(s):
        e = jnp.exp(s - jnp.max(s, axis=-1, keepdims=True))
        ov = _dot(e.astype(BF16), vs_ref[...])
        return ov[:, :LANES] / ov[:, LANES:]

    def scores(m, i):
        return _dot_nt(qs_ref[m, i * tq:(i + 1) * tq, :], ks_ref[...])

    nq = S // tq
    s1, s2 = scores(0, 0), scores(1, 0)
    for i in range(nq):
        last = i + 1 == nq
        s1_next = None if last else scores(0, i + 1)
        o1 = softmax_pv(s1)
        s2_next = None if last else scores(1, i + 1)
        o = o1 - lam * softmax_pv(s2)
        o = _rms(o, w, SUBLN_EPS) * (1.0 - lambda_init)
        o_ref[i * tq:(i + 1) * tq, :] = o.astype(o_ref.dtype)
        s1, s2 = s1_next, s2_next


def _diff_attention(proj, rope_c, rope_sa, rope_sb, lam_params, subln_w, heads, lambda_init, tq=512):
    B, S, _ = proj.shape
    tq = min(tq, S)
    head = lambda off: pl.BlockSpec((None, S, LANES), lambda b, h: (b, 0, off + h))
    table = pl.BlockSpec((None, S, LANES), lambda b, h: (b, 0, 0))
    return pl.pallas_call(
        functools.partial(_diff_attn_kernel, tq=tq, lambda_init=lambda_init),
        grid=(B, heads),
        in_specs=[pl.BlockSpec((4, DIFF_QKDIM), lambda b, h: (0, 0)),
                  head(0), head(heads), head(2 * heads), table, table, table,
                  pl.BlockSpec((1, DIFF_VDIM), lambda b, h: (0, 0))],
        out_specs=pl.BlockSpec((None, S, LANES), lambda b, h: (b, 0, h)),
        out_shape=jax.ShapeDtypeStruct((B, S, heads * DIFF_VDIM), BF16),
        scratch_shapes=[pltpu.VMEM((2, S, LANES), BF16), pltpu.VMEM((S, LANES), BF16),
                        pltpu.VMEM((S, 2 * LANES), BF16)],
        compiler_params=_cparams("parallel", "parallel"),
        name="diff_attention",
    )(lam_params, proj, proj, proj, rope_c, rope_sa, rope_sb, subln_w.reshape(1, DIFF_VDIM))


def _bdot(a, b):
    return lax.dot_general(a, b, (((2,), (1,)), ((0,), (0,))), preferred_element_type=F32)


def _bdot_nt(a, b):
    return lax.dot_general(a, b, (((2,), (2,)), ((0,), (0,))), preferred_element_type=F32)


def _bdot_tn(a, b):
    return lax.dot_general(a, b, (((1,), (1,)), ((0,), (0,))), preferred_element_type=F32)


def _unit_tri_inverse(a, eye, same_block):
    mm = lambda x, y: _bdot(x.astype(BF16), y.astype(BF16))
    ad = jnp.where(same_block, a, 0.0)
    ao = a - ad
    p = eye - ad
    sq = ad
    for _ in range(int(math.log2(DN_BLOCK)) - 1):
        sq = mm(sq, sq)
        p = p + mm(p, sq)
    b = mm(p, ao)
    b2 = mm(b, b)
    x = eye - b + b2 - mm(b, b2)
    return mm(x, p)


def _gdn_kernel(alog_ref, dtb_ref, q_ref, k_ref, v_ref, z_ref, cq_ref, ck_ref, cv_ref,
                gcol_ref, grow_ref, nw_ref, o_ref,
                q16_ref, k16_ref, gc_ref, beta_ref, rhs_ref, kt_ref, egl_ref, gcr_ref,
                u_ref, lhs_ref, qkd_ref, qm_ref, vn_ref, od_ref):
    S = q_ref.shape[0]
    C = DN_CHUNK
    N = S // C
    h = pl.program_id(1)

    t = lax.broadcasted_iota(jnp.int32, (S, LANES), 0)
    pad = (DN_CONV - 1) // 2

    def conv_silu(x_ref, cw_ref):
        x = x_ref[...].astype(F32)
        cw = cw_ref[...].astype(F32)
        y = x * cw[pad:pad + 1]
        for i in range(DN_CONV):
            d = i - pad
            if d == 0:
                continue
            xs = pltpu.roll(x, (-d) % S, 0)
            valid = (t + d >= 0) & (t + d < S)
            y = y + jnp.where(valid, xs, 0.0) * cw[i:i + 1]
        return _silu(y)

    def l2n(x):
        return x * lax.rsqrt(jnp.sum(x * x, axis=-1, keepdims=True) + 1e-6)

    qn = l2n(conv_silu(q_ref, cq_ref)) * (DN_KDIM ** -0.5)
    kn = l2n(conv_silu(k_ref, ck_ref))
    vn = conv_silu(v_ref, cv_ref)
    q16_ref[...] = qn.astype(BF16)
    k16_ref[...] = kn.astype(BF16)

    ri = lax.broadcasted_iota(jnp.int32, (C, C), 0)
    ci = lax.broadcasted_iota(jnp.int32, (C, C), 1)
    eye = (ri == ci).astype(F32)
    same_block = (ri // DN_BLOCK) == (ci // DN_BLOCK)
    incl = (ri >= ci, ri <= ci)
    strict = (ri > ci, ri < ci)
    tri_row = (incl[1].astype(F32), incl[0].astype(F32))

    def softplus(x):
        return jnp.maximum(x, 0.0) + jnp.log1p(jnp.exp(-jnp.abs(x)))

    def log_decay(x, d):
        a = jnp.exp(jnp.full((1, 1), alog_ref[d, h], F32))
        return -a * softplus(x + dtb_ref[d, h])

    pos = t % C
    grp = lax.broadcasted_iota(jnp.int32, (1, LANES), 1) // GATE_REP
    per_dir = lambda ref: jnp.where(grp == 0, ref[0, h], jnp.where(grp == 2, ref[1, h], 0.0))
    chunked = lambda m: m.reshape(N, C, LANES)
    gx = gcol_ref[...].astype(F32)
    g = -jnp.exp(per_dir(alog_ref)) * softplus(gx + per_dir(dtb_ref))
    beta_x = jax.nn.sigmoid(gx)
    pre = g
    shift = 1
    while shift < C:
        pre = pre + jnp.where(pos >= shift, pltpu.roll(pre, shift, 0), 0.0)
        shift *= 2
    pre3 = chunked(pre)
    tot = pre3[:, C - 1:C, :]
    gc_x = jnp.where(grp == 2, tot - pre3 + chunked(g), pre3).reshape(S, LANES)
    from_group = lambda m, q: jnp.broadcast_to(m[:, q * GATE_REP:q * GATE_REP + 1], (S, LANES))
    for d in range(2):
        beta = from_group(beta_x, 2 * d + 1)
        gc = from_group(gc_x, 2 * d)
        gc3 = chunked(gc)
        g_last = gc3[:, C - 1:C, :] if d == 0 else gc3[:, 0:1, :]
        egc = jnp.exp(gc)
        gc_ref[d] = gc
        beta_ref[d] = beta
        rhs_ref[d, :, :LANES] = (vn * beta).astype(BF16)
        rhs_ref[d, :, LANES:] = (kn * beta * egc).astype(BF16)
        kt_ref[d] = (chunked(kn) * jnp.exp(g_last - gc3)).reshape(S, LANES).astype(BF16)
        lhs_ref[d, :, DN_KDIM + C:, :] = chunked(qn * egc).astype(BF16)
        egl_ref[d] = jnp.broadcast_to(jnp.exp(g_last), (N, 8, LANES))
        g_row = log_decay(grow_ref[2 * d].astype(F32), d)
        gc_row = _dot_exact(g_row, tri_row[d])
        for n in range(N):
            gcr_ref[d, n] = jnp.broadcast_to(gc_row[n:n + 1, :], (8, C))

    U = min(DN_UNROLL, N)

    def factor_body(i, carry):
        n0 = pl.multiple_of(i * U, U)
        rows = pl.ds(pl.multiple_of(i * (U * C), U * C), U * C)
        chunks = pl.ds(n0, U)
        per_chunk = lambda m: m.reshape(U, C, m.shape[-1])
        k16 = per_chunk(k16_ref[rows, :])
        kq = _bdot_nt(jnp.concatenate([k16, per_chunk(q16_ref[rows, :])], axis=1), k16)
        kk, qk = kq[:, :C], kq[:, C:]
        a = []
        for d in range(2):
            gcc = per_chunk(gc_ref[d, rows, :])[:, :, :C]
            diff = jnp.where(incl[d], gcc - gcr_ref[d, chunks][:, 0:1, :], 0.0)
            decay = jnp.where(incl[d], jnp.exp(diff), 0.0)
            qkd_ref[d, chunks] = (qk * decay).astype(BF16)
            a.append(jnp.where(strict[d], per_chunk(beta_ref[d, rows, :])[:, :, :C] * kk * decay, 0.0))
        tinv = _unit_tri_inverse(jnp.concatenate(a, axis=0), eye, same_block)
        rhs = jnp.concatenate([per_chunk(rhs_ref[d, rows, :]) for d in range(2)], axis=0)
        uw = _bdot(tinv.astype(BF16), rhs)
        uw16 = uw.astype(BF16)
        kt = jnp.concatenate([per_chunk(kt_ref[d, rows, :]) for d in range(2)], axis=0)
        pq = _bdot_tn(kt, uw16)
        for d in range(2):
            sel = slice(d * U, (d + 1) * U)
            u_ref[d, rows, :] = uw[sel, :, :LANES].reshape(U * C, LANES)
            qm_ref[d, chunks] = pq[sel, :, :LANES]
            lhs_ref[d, chunks, :DN_KDIM, :] = pq[sel, :, LANES:].astype(BF16)
            lhs_ref[d, chunks, DN_KDIM:DN_KDIM + C, :] = uw16[sel, :, LANES:]
        return carry

    lax.fori_loop(0, N // U, factor_body, 0)

    def scan_step(d, n, state):
        rows = pl.ds(pl.multiple_of(n * C, C), C)
        prod = _dot(lhs_ref[d, n], state.astype(BF16))
        vn_ref[d, rows, :] = (u_ref[d, rows, :] - prod[DN_KDIM:DN_KDIM + C]).astype(BF16)
        od_ref[d, rows, :] = prod[DN_KDIM + C:]
        return state * egl_ref[d, n][0:1, :] - prod[:DN_KDIM] + qm_ref[d, n]

    def scan_body(n, states):
        return scan_step(0, n, states[0]), scan_step(1, N - 1 - n, states[1])

    zero = jnp.zeros((DN_KDIM, DN_VDIM), F32)
    lax.fori_loop(0, N, scan_body, (zero, zero), unroll=2)

    def intra_body(i, carry):
        rows = pl.ds(pl.multiple_of(i * (U * C), U * C), U * C)
        chunks = pl.ds(pl.multiple_of(i * U, U), U)
        for d in range(2):
            intra = _bdot(qkd_ref[d, chunks], vn_ref[d, rows, :].reshape(U, C, LANES))
            od_ref[d, rows, :] += intra.reshape(U * C, LANES)
        return carry

    lax.fori_loop(0, N // U, intra_body, 0)

    o = od_ref[0] + od_ref[1]
    o = _rms(o, nw_ref[...], NORM_EPS) * _silu(z_ref[...].astype(F32))
    o_ref[...] = o.astype(o_ref.dtype)


def _gated_deltanet(proj, qkv_block0, z_block0, conv_w, gates_col, gates_row, a_log, dt_bias, norm_w, heads):
    B, S, _ = proj.shape
    N = S // DN_CHUNK
    head = lambda off: pl.BlockSpec((None, S, LANES), lambda b, h: (b, 0, off + h))
    cw = lambda off: pl.BlockSpec((DN_CONV, LANES), lambda b, h: (0, off + h))
    smem = pl.BlockSpec(memory_space=pltpu.SMEM)
    return pl.pallas_call(
        _gdn_kernel,
        grid=(B, heads),
        in_specs=[smem, smem,
                  head(qkv_block0), head(qkv_block0 + heads), head(qkv_block0 + 2 * heads), head(z_block0),
                  cw(0), cw(heads), cw(2 * heads),
                  pl.BlockSpec((None, None, S, LANES), lambda b, h: (b, h, 0, 0)),
                  pl.BlockSpec((None, None, 4, N, DN_CHUNK), lambda b, h: (b, h, 0, 0, 0)),
                  pl.BlockSpec((1, DN_VDIM), lambda b, h: (0, 0))],
        out_specs=pl.BlockSpec((None, S, LANES), lambda b, h: (b, 0, h)),
        out_shape=jax.ShapeDtypeStruct((B, S, heads * DN_VDIM), BF16),
        scratch_shapes=[
            pltpu.VMEM((S, LANES), BF16),
            pltpu.VMEM((S, LANES), BF16),
            pltpu.VMEM((2, S, LANES), F32),
            pltpu.VMEM((2, S, LANES), F32),
            pltpu.VMEM((2, S, 2 * LANES), BF16),
            pltpu.VMEM((2, S, LANES), BF16),
            pltpu.VMEM((2, N, 8, LANES), F32),
            pltpu.VMEM((2, N, 8, DN_CHUNK), F32),
            pltpu.VMEM((2, S, LANES), F32),
            pltpu.VMEM((2, N, DN_KDIM + 2 * DN_CHUNK, LANES), BF16),
            pltpu.VMEM((2, N, DN_CHUNK, DN_CHUNK), BF16),
            pltpu.VMEM((2, N, DN_KDIM, LANES), F32),
            pltpu.VMEM((2, S, LANES), BF16),
            pltpu.VMEM((2, S, LANES), F32),
        ],
        compiler_params=_cparams("parallel", "parallel"),
        name="gated_deltanet",
    )(a_log, dt_bias, proj, proj, proj, proj, conv_w, conv_w, conv_w, gates_col, gates_row,
      norm_w.reshape(1, DN_VDIM))


def _cross_attn_kernel(q_ref, kv_ref, o_ref):
    width = MEM_HEADS * MEM_HDIM
    outs = []
    for hd in range(MEM_HEADS):
        cols = slice(hd * MEM_HDIM, (hd + 1) * MEM_HDIM)
        q = q_ref[:, cols]
        k = kv_ref[:, cols]
        v = kv_ref[:, width + hd * MEM_HDIM:width + (hd + 1) * MEM_HDIM]
        s = _dot_nt(q, k) * (MEM_HDIM ** -0.5)
        e = jnp.exp(s - jnp.max(s, axis=-1, keepdims=True))
        l = jnp.sum(e, axis=-1, keepdims=True)
        outs.append(_dot(e.astype(BF16), v) / l)
    o_ref[...] = jnp.concatenate(outs, axis=1).astype(o_ref.dtype)


def _cross_attention(q, kv, tq=512):
    B, S, W = q.shape
    M = kv.shape[1]
    tq = min(tq, S)
    return pl.pallas_call(
        _cross_attn_kernel,
        grid=(B, S // tq),
        in_specs=[pl.BlockSpec((None, tq, W), lambda b, i: (b, i, 0)),
                  pl.BlockSpec((None, M, 2 * W), lambda b, i: (b, 0, 0))],
        out_specs=pl.BlockSpec((None, tq, W), lambda b, i: (b, i, 0)),
        out_shape=jax.ShapeDtypeStruct((B, S, W), BF16),
        compiler_params=_cparams("parallel", "parallel"),
        name="cross_attention",
    )(q, kv)


def _rope_tables(positions):
    B, S = positions.shape
    half = ROPE_DIM // 2
    inv_freq = ROPE_THETA ** (-jnp.arange(0, ROPE_DIM, 2, dtype=F32) / ROPE_DIM)
    ang = positions.astype(F32)[..., None] * inv_freq
    cos, sin = jnp.cos(ang), jnp.sin(ang)
    rest = DIFF_QKDIM - ROPE_DIM
    ones, zeros = jnp.ones((B, S, rest), F32), jnp.zeros((B, S, rest), F32)
    z8 = jnp.zeros((B, S, half), F32)
    both = lambda m: jnp.concatenate([m, m], axis=-1)
    c = both(jnp.concatenate([cos, cos, ones], axis=-1))
    sa = both(jnp.concatenate([z8, sin, zeros], axis=-1))
    sb = both(jnp.concatenate([-sin, z8, zeros], axis=-1))
    return c, sa, sb


def _pad_cols(w, n):
    return jnp.pad(w, ((0, 0), (0, n - w.shape[1])))


def _ffn_weights(w_gu, w_down):
    d_ff = w_down.shape[0]
    d_pad = -(-d_ff // FF_ALIGN) * FF_ALIGN
    assert d_ff % FF_PREP_BLOCK == 0
    n_real, n_padded = d_ff // FF_PREP_BLOCK, d_pad // FF_PREP_BLOCK
    return (_cast_pad_halves(w_gu, 2, n_real, n_padded, FF_PREP_BLOCK),
            _cast_pad_rows(w_down, n_real, n_padded, FF_PREP_BLOCK))


def kernel(x, mem, positions, ffn1_norms, ffn1_w_gu, ffn1_w_down, mix_norms, mix_w_in, dn_conv_w, dn_a_log, dn_dt_bias, dn_norm_w, diff_lambda, diff_subln_w, mix_w_out, mem_norms, mem_w_q, mem_w_kv, mem_w_o, ffn2_norms, ffn2_w_gu, ffn2_w_down):
    B, S, D = x.shape
    T = B * S
    depth = ffn1_norms.shape[0]
    diff_width = D // 2
    dn_width = D - diff_width
    diff_heads = diff_width // DIFF_VDIM
    dn_heads = dn_width // DN_VDIM
    main_cols = 3 * diff_width + 4 * dn_width
    n_chunks = S // DN_CHUNK

    rope_c, rope_sa, rope_sb = _rope_tables(positions)
    xt = x.reshape(T, D)
    h = _norm_cast(xt, ffn1_norms[0, 0])
    for l in range(depth):
        lambda_init = 0.8 - 0.6 * math.exp(-0.3 * l)
        y = _ffn(h, *_ffn_weights(ffn1_w_gu[l], ffn1_w_down[l]))
        xt, h = _post_pre(y, xt, ffn1_norms[l, 1], mix_norms[l, 0], 0.5)
        w_in = mix_w_in[l].astype(BF16)
        proj = _matmul(h, w_in, BF16, tm=1024, tn=1024, n_cols=main_cols, name="matmul_in")
        gates = _matmul(h, _pad_cols(w_in[:, main_cols:], LANES), F32, tm=1024, tn=LANES,
                        name="matmul_gates")
        proj = proj.reshape(B, S, main_cols)
        gates = gates.reshape(B, S, LANES)[:, :, :4 * dn_heads].reshape(B, S, 4, dn_heads)
        gates_col = jnp.repeat(gates.transpose(0, 3, 1, 2), GATE_REP, axis=-1)
        gates_row = gates.transpose(0, 3, 2, 1).reshape(B, dn_heads, 4, n_chunks, DN_CHUNK)
        o_diff = _diff_attention(proj, rope_c, rope_sa, rope_sb, diff_lambda[l], diff_subln_w[l],
                                 diff_heads, lambda_init)
        o_dn = _gated_deltanet(proj, 3 * diff_heads, 3 * diff_heads + 3 * dn_heads, dn_conv_w[l],
                               gates_col, gates_row, dn_a_log[l], dn_dt_bias[l], dn_norm_w[l], dn_heads)
        w_out = mix_w_out[l]
        mixed = _matmul_dual(o_diff.reshape(T, diff_width), o_dn.reshape(T, dn_width),
                             w_out[:diff_width].astype(BF16), w_out[diff_width:].astype(BF16),
                             tm=1024, tn=1024)
        xt, h = _post_pre(mixed, xt, mix_norms[l, 1], mem_norms[l, 0], 1.0)
        mlen = mem.shape[1]
        memn = _norm_cast(mem.reshape(B * mlen, D), mem_norms[l, 1])
        kv = _matmul(memn, mem_w_kv[l].astype(BF16), BF16, tm=1024, tn=1024, name="matmul_mem_kv")
        q = _matmul(h, mem_w_q[l].astype(BF16), BF16, tm=1024, tn=512, name="matmul_mem_q")
        width = MEM_HEADS * MEM_HDIM
        o_mem = _cross_attention(q.reshape(B, S, width), kv.reshape(B, mlen, 2 * width))
        c = _matmul(o_mem.reshape(T, width), mem_w_o[l].astype(BF16), BF16, tm=1024, tn=1024,
                    name="matmul_mem_o")
        xt, h = _post_pre(c, xt, mem_norms[l, 2], ffn2_norms[l, 0], 1.0)
        y = _ffn(h, *_ffn_weights(ffn2_w_gu[l], ffn2_w_down[l]))
        if l + 1 < depth:
            xt, h = _post_pre(y, xt, ffn2_norms[l, 1], ffn1_norms[l + 1, 0], 0.5)
        else:
            xt = _post_pre(y, xt, ffn2_norms[l, 1], None, 0.5)
    return xt.reshape(B, S, D)
```
